```python
import math
import jax
import jax.numpy as jnp
from jax import lax
import numpy as np


D_MODEL = 2048
BATCH = 4
SEQ = 2048
DEPTH = 2
DEC_BATCH = 128
DEC_SEQ = 8
PAST_LEN = 16384
PAGE_SIZE = 128

N_BRANCH = 4
BRANCH_WIDTH = D_MODEL // N_BRANCH
GDN_HEADS = 4
GDN_HEAD_DIM = BRANCH_WIDTH // GDN_HEADS
GDN_CONV = 4
GDN_CHUNK = 64
DIFF_HEADS = 4
DIFF_HEAD_DIM = BRANCH_WIDTH // (2 * DIFF_HEADS)
S5_GROUP_CH = 16
S5_GROUPS = BRANCH_WIDTH // S5_GROUP_CH
S5_STATE = 64
MLA_HEADS = 4
MLA_NOPE = 64
MLA_ROPE = 32
MLA_V = BRANCH_WIDTH // MLA_HEADS
MLA_Q_LORA = 384
MLA_KV_LORA = 256
ROPE_THETA = 10000.0
D_FF = 5632
FFN_CONV = 3
Q_BLOCK = 128
PAGES_PER_BLOCK = 16
NEG_INF = -1e30
EPS = 1e-6
IN_SPLITS = (3 * GDN_HEADS * GDN_HEAD_DIM, GDN_HEADS * GDN_HEAD_DIM, GDN_HEADS, GDN_HEADS,
             DIFF_HEADS * 2 * DIFF_HEAD_DIM, 2 * DIFF_HEAD_DIM, 2 * DIFF_HEAD_DIM,
             BRANCH_WIDTH, MLA_Q_LORA, MLA_KV_LORA, MLA_ROPE, N_BRANCH * D_MODEL)
STATE_NAMES = ('diff_k', 'diff_v', 'mla_ckv', 'mla_krope', 'gdn', 'gdn_conv', 's5_re', 's5_im', 'ffn_conv')

kernel_name = 'hybrid_gdn_diffattn_s5_mla_decode_step'


def _split(t, sizes):
    out, start = [], 0
    for s in sizes:
        out.append(t[..., start:start + s])
        start += s
    return out


def _rmsnorm(x, g):
    xf = x.astype(jnp.float32)
    y = xf * lax.rsqrt(jnp.mean(xf * xf, axis=-1, keepdims=True) + EPS)
    return (y * g.astype(jnp.float32)).astype(x.dtype)


def _l2norm(x):
    xf = x.astype(jnp.float32)
    return xf * lax.rsqrt(jnp.sum(xf * xf, axis=-1, keepdims=True) + EPS)


def _rope(x, pos):
    half = x.shape[-1] // 2
    inv = ROPE_THETA ** (-jnp.arange(half, dtype=jnp.float32) / half)
    ang = pos.astype(jnp.float32)[:, None] * inv
    ang = ang.reshape((pos.shape[0],) + (1,) * (x.ndim - 3) + (half,))
    cos, sin = jnp.cos(ang), jnp.sin(ang)
    xf = x.astype(jnp.float32)
    x1, x2 = xf[..., :half], xf[..., half:]
    return jnp.concatenate([x1 * cos - x2 * sin, x2 * cos + x1 * sin], axis=-1).astype(x.dtype)


def _alibi_slopes(n_heads):
    return 2.0 ** (-8.0 * jnp.arange(1, n_heads + 1, dtype=jnp.float32) / n_heads)


def _causal_dwconv(x, buf, w):
    width, length = w.shape[0], x.shape[1]
    xx = jnp.concatenate([buf.astype(x.dtype), x], axis=1)
    y = w[0] * xx[:, :length]
    for j in range(1, width):
        y = y + w[j] * xx[:, j:j + length]
    return y, xx[:, length:]


def _gated_delta_rule(q, k, v, g, beta, s0):
    b, length, h, _ = q.shape
    c = math.gcd(length, GDN_CHUNK)
    n = length // c

    def chunks(t):
        t = t.reshape((b, n, c, h) + t.shape[3:])
        return jnp.moveaxis(t, (1, 3), (0, 2))

    qc, kc, vc, bc = chunks(q), chunks(k), chunks(v), chunks(beta)
    gc = jnp.cumsum(chunks(g), axis=-1)
    tril = jnp.tril(jnp.ones((c, c), dtype=bool))
    strict = jnp.tril(jnp.ones((c, c), dtype=bool), k=-1)
    diff = gc[..., :, None] - gc[..., None, :]
    decay = jnp.where(tril, jnp.exp(jnp.where(tril, diff, 0.0)), 0.0)
    k_beta = kc * bc[..., None]
    a = jnp.where(strict, jnp.einsum('nbhid,nbhjd->nbhij', k_beta, kc) * decay, 0.0)
    lmat = a + jnp.eye(c, dtype=jnp.float32)
    u = lax.linalg.triangular_solve(lmat, vc * bc[..., None], left_side=True, lower=True, unit_diagonal=True)
    w = lax.linalg.triangular_solve(lmat, k_beta * jnp.exp(gc)[..., None], left_side=True, lower=True, unit_diagonal=True)
    qk = jnp.where(tril, jnp.einsum('nbhid,nbhjd->nbhij', qc, kc) * decay, 0.0)

    def step(s, xs):
        q_i, k_i, u_i, w_i, g_i, qk_i = xs
        v_new = u_i - jnp.einsum('bhcd,bhdv->bhcv', w_i, s)
        o = (jnp.einsum('bhcd,bhdv->bhcv', q_i * jnp.exp(g_i)[..., None], s)
             + jnp.einsum('bhij,bhjv->bhiv', qk_i, v_new))
        g_last = g_i[..., -1]
        k_dec = k_i * jnp.exp(g_last[..., None] - g_i)[..., None]
        s = s * jnp.exp(g_last)[..., None, None] + jnp.einsum('bhcd,bhcv->bhdv', k_dec, v_new)
        return s, o

    s_final, o = lax.scan(step, s0, (qc, kc, u, w, gc, qk))
    o = jnp.moveaxis(o, (0, 2), (1, 3)).reshape(b, length, h, -1)
    return o, s_final


def _linear_combine(e1, e2):
    a1, b1 = e1
    a2, b2 = e2
    return a1 * a2, a2 * b1 + b2


def _s5(u, x0_re, x0_im, p):
    b, length, _ = u.shape
    f32 = jnp.float32
    uf = u.astype(f32).reshape(b, length, S5_GROUPS, S5_GROUP_CH)
    lam = lax.complex(p['s5_a_re'].astype(f32), p['s5_a_im'].astype(f32))
    dt = jnp.exp(p['s5_log_dt'].astype(f32))[:, None]
    lam_bar = jnp.exp(lam * dt)
    b_bar = ((lam_bar - 1.0) / lam)[..., None] * lax.complex(p['s5_b_re'].astype(f32), p['s5_b_im'].astype(f32))
    c_mat = lax.complex(p['s5_c_re'].astype(f32), p['s5_c_im'].astype(f32))
    bu = jnp.einsum('gpc,blgc->blgp', b_bar, uf.astype(jnp.complex64))
    x0 = lax.complex(x0_re.astype(f32), x0_im.astype(f32))
    bu = bu.at[:, 0].add(lam_bar * x0)
    a = jnp.broadcast_to(lam_bar, bu.shape)
    _, states = lax.associative_scan(_linear_combine, (a, bu), axis=1)
    y = (jnp.einsum('gcp,blgp->blgc', c_mat, states).real
         + p['s5_d'].astype(f32).reshape(S5_GROUPS, S5_GROUP_CH) * uf)
    v = jax.nn.gelu(y.reshape(b, length, BRANCH_WIDTH)).astype(u.dtype)
    glu_a, glu_b = jnp.split(v @ p['s5_w_glu'], 2, axis=-1)
    last = states[:, -1]
    return glu_a * jax.nn.sigmoid(glu_b), last.real.astype(x0_re.dtype), last.imag.astype(x0_im.dtype)


def _attn_init(b, m, h, t, dv):
    return (jnp.full((b, m, h, t), NEG_INF, jnp.float32),
            jnp.zeros((b, m, h, t), jnp.float32),
            jnp.zeros((b, m, h, t, dv), jnp.float32))


def _attn_step(carry, q, k, v, q_pos, k_pos, slopes, scale):
    m_old, l_old, acc = carry
    s = jnp.einsum('btmhd,bsmd->bmhts', q, k, preferred_element_type=jnp.float32) * scale
    rel = q_pos[:, None] - k_pos[None, :]
    if slopes is not None:
        s = s - slopes[:, None, None] * rel.astype(jnp.float32)
    s = jnp.where(rel >= 0, s, NEG_INF)
    m_new = jnp.maximum(m_old, jnp.max(s, axis=-1))
    pr = jnp.exp(s - m_new[..., None])
    corr = jnp.exp(m_old - m_new)
    l_new = l_old * corr + jnp.sum(pr, axis=-1)
    acc = acc * corr[..., None] + jnp.einsum('bmhts,bsv->bmhtv', pr, v.astype(jnp.float32))
    return m_new, l_new, acc


def _prompt_attention(q, k, v, slopes, scale):
    b, length, m, h, dk = q.shape
    qb = min(Q_BLOCK, length)
    nb = length // qb
    q_blocks = q.reshape(b, nb, qb, m, h, dk).swapaxes(0, 1)
    k_pos = jnp.arange(length, dtype=jnp.int32)

    def one(args):
        q_blk, j = args
        q_pos = j * qb + jnp.arange(qb, dtype=jnp.int32)
        _, l_sum, acc = _attn_step(_attn_init(b, m, h, qb, v.shape[-1]), q_blk, k, v, q_pos, k_pos, slopes, scale)
        return acc / l_sum[..., None]

    out = lax.map(one, (q_blocks, jnp.arange(nb, dtype=jnp.int32)))
    return out.transpose(1, 0, 4, 2, 3, 5).reshape(b, length, m, h, -1).astype(v.dtype)


def _paged_attention(q, k_new, v_new, q_pos, slopes, scale, gather, page_table):
    b, t, m, h, _ = q.shape
    n_pages = page_table.shape[1]
    pb = math.gcd(n_pages, PAGES_PER_BLOCK)
    n_blk = n_pages // pb
    blk_len = pb * PAGE_SIZE
    pt = page_table.reshape(b, n_blk, pb).swapaxes(0, 1)

    def body(carry, xs):
        pages, j = xs
        kb, vb = gather(pages)
        k_pos = j * blk_len + jnp.arange(blk_len, dtype=jnp.int32)
        return _attn_step(carry, q, kb, vb, q_pos, k_pos, slopes, scale), None

    carry, _ = lax.scan(body, _attn_init(b, m, h, t, v_new.shape[-1]), (pt, jnp.arange(n_blk, dtype=jnp.int32)))
    _, l_sum, acc = _attn_step(carry, q, k_new, v_new, q_pos, q_pos, slopes, scale)
    return (acc / l_sum[..., None]).transpose(0, 3, 1, 2, 4).astype(v_new.dtype)


def _make_paged_attend(i, q_pos, page_table, cache_diff_k, cache_diff_v, cache_mla_ckv, cache_mla_krope):
    def gather_diff(pages):
        b, n = pages.shape
        kk = cache_diff_k[i, pages].reshape(b, n * PAGE_SIZE, 2, DIFF_HEAD_DIM)
        vv = cache_diff_v[i, pages].reshape(b, n * PAGE_SIZE, 2 * DIFF_HEAD_DIM)
        return kk, vv

    def gather_mla(pages):
        b, n = pages.shape
        ckv = cache_mla_ckv[i, pages].reshape(b, n * PAGE_SIZE, MLA_KV_LORA)
        kr = cache_mla_krope[i, pages].reshape(b, n * PAGE_SIZE, MLA_ROPE)
        return jnp.concatenate([ckv, kr], axis=-1)[:, :, None], ckv

    gathers = {'diff': gather_diff, 'mla': gather_mla}

    def attend(kind, q, k, v, slopes, scale):
        return _paged_attention(q, k, v, q_pos, slopes, scale, gathers[kind], page_table)
    return attend


def _prompt_attend(kind, q, k, v, slopes, scale):
    return _prompt_attention(q, k, v, slopes, scale)


def _empty_state(b, dtype):
    return dict(gdn=jnp.zeros((b, GDN_HEADS, GDN_HEAD_DIM, GDN_HEAD_DIM), dtype),
                gdn_conv=jnp.zeros((b, GDN_CONV - 1, 3 * GDN_HEADS * GDN_HEAD_DIM), dtype),
                s5_re=jnp.zeros((b, S5_GROUPS, S5_STATE), dtype),
                s5_im=jnp.zeros((b, S5_GROUPS, S5_STATE), dtype),
                ffn_conv=jnp.zeros((b, FFN_CONV - 1, 2 * D_FF), dtype))


def _layer(i, x, c, pos, st, p, attend):
    f32 = jnp.float32
    b, length, _ = x.shape
    mod = jnp.einsum('bd,de->be', jax.nn.silu(c), p['w_ada']) + p['b_ada']
    sh_m, sc_m, gt_m, sh_f, sc_f, gt_f = jnp.split(mod[:, None, :], 6, axis=-1)
    h = _rmsnorm(x, p['norm_mix']) * (1 + sc_m) + sh_m
    (g_qkv, g_z, g_b, g_a, d_q, d_k, d_v, s_u, m_qa, m_kva, m_kr, gates) = _split(h @ p['w_in'], IN_SPLITS)

    qkv, gdn_conv_new = _causal_dwconv(g_qkv, st['gdn_conv'], p['gdn_conv_w'])
    qkv = jax.nn.silu(qkv).reshape(b, length, 3, GDN_HEADS, GDN_HEAD_DIM)
    gq = _l2norm(qkv[:, :, 0]) * GDN_HEAD_DIM ** -0.5
    gk = _l2norm(qkv[:, :, 1])
    gv = qkv[:, :, 2].astype(f32)
    beta = jax.nn.sigmoid(g_b.astype(f32))
    log_decay = -jnp.exp(p['gdn_a_log'].astype(f32)) * jax.nn.softplus(g_a.astype(f32) + p['gdn_dt_bias'].astype(f32))
    o_a, gdn_state_new = _gated_delta_rule(gq, gk, gv, log_decay, beta, st['gdn'].astype(f32))
    o_a = _rmsnorm(o_a, p['gdn_norm']) * jax.nn.silu(g_z.reshape(b, length, GDN_HEADS, GDN_HEAD_DIM).astype(f32))
    y_a = o_a.reshape(b, length, BRANCH_WIDTH).astype(x.dtype)

    dq = _rmsnorm(d_q.reshape(b, length, DIFF_HEADS, 2, DIFF_HEAD_DIM).swapaxes(2, 3), p['diff_q_norm'])
    dk = _rmsnorm(d_k.reshape(b, length, 2, DIFF_HEAD_DIM), p['diff_k_norm'])
    lam_init = 0.8 - 0.6 * math.exp(-0.3 * i)
    lam = (jnp.exp(jnp.sum(p['diff_lq1'].astype(f32) * p['diff_lk1'].astype(f32)))
           - jnp.exp(jnp.sum(p['diff_lq2'].astype(f32) * p['diff_lk2'].astype(f32))) + lam_init)
    att_b = attend('diff', dq, dk, d_v, _alibi_slopes(DIFF_HEADS), DIFF_HEAD_DIM ** -0.5)
    o_b = att_b[:, :, 0].astype(f32) - lam * att_b[:, :, 1].astype(f32)
    o_b = _rmsnorm(o_b, p['diff_subln']) * (1.0 - lam_init)
    y_b = o_b.reshape(b, length, BRANCH_WIDTH).astype(x.dtype)

    y_c, s5_re_new, s5_im_new = _s5(s_u, st['s5_re'], st['s5_im'], p)
    y_c = y_c.astype(x.dtype)

    mq = _rmsnorm(m_qa, p['mla_q_a_norm']) @ p['mla_w_qb']
    mq = _rmsnorm(mq.reshape(b, length, MLA_HEADS, MLA_NOPE + MLA_ROPE), p['mla_q_norm'])
    q_rope = _rope(mq[..., MLA_NOPE:], pos)
    q_lat = jnp.einsum('blhd,rhd->blhr', mq[..., :MLA_NOPE], p['mla_w_kvb'][..., :MLA_NOPE])
    ckv = _rmsnorm(m_kva, p['mla_kv_norm'])
    krope = _rope(_rmsnorm(m_kr, p['mla_k_norm']), pos)
    q_cat = jnp.concatenate([q_lat, q_rope.astype(q_lat.dtype)], axis=-1)[:, :, None]
    k_cat = jnp.concatenate([ckv, krope], axis=-1)[:, :, None]
    att_d = attend('mla', q_cat, k_cat, ckv, None, (MLA_NOPE + MLA_ROPE) ** -0.5)
    y_d = jnp.einsum('blhr,rhv->blhv', att_d[:, :, 0], p['mla_w_kvb'][..., MLA_NOPE:]).reshape(b, length, BRANCH_WIDTH)
    y_d = y_d.astype(x.dtype)

    ys = jnp.stack([y_a, y_b, y_c, y_d], axis=2)
    z = jnp.einsum('blnk,nkd->blnd', ys, p['w_branch'])
    gate = jax.nn.sigmoid(gates.reshape(b, length, N_BRANCH, D_MODEL))
    x = x + gt_m * (jnp.sum(gate * z, axis=2) @ p['w_out'])

    hf = _rmsnorm(x, p['norm_ffn']) * (1 + sc_f) + sh_f
    up, ffn_conv_new = _causal_dwconv(hf @ p['ffn_w_up'], st['ffn_conv'], p['ffn_conv_w'])
    u_gate, u_val = jnp.split(up, 2, axis=-1)
    x = x + gt_f * ((jax.nn.silu(u_gate) * u_val) @ p['ffn_w_down'])

    new = dict(diff_k=dk, diff_v=d_v, mla_ckv=ckv, mla_krope=krope,
               gdn=gdn_state_new.astype(st['gdn'].dtype), gdn_conv=gdn_conv_new,
               s5_re=s5_re_new, s5_im=s5_im_new, ffn_conv=ffn_conv_new)
    return x, new


def setup_inputs(seed: int = 0) -> dict:
    key = jax.random.key(seed)
    keys = iter(jax.random.split(key, 80))
    f32 = jnp.float32

    def nrm(shape, scale=1.0):
        return jax.random.normal(next(keys), shape, f32) * scale

    def gain(shape):
        return 1.0 + 0.02 * jax.random.normal(next(keys), shape, f32)

    def unif(shape, lo, hi):
        return jax.random.uniform(next(keys), shape, f32, lo, hi)

    n_pages = PAST_LEN // PAGE_SIZE
    n_pool = (DEC_BATCH * n_pages * 5) // 4
    d_in = sum(IN_SPLITS)
    qkv_width = 3 * GDN_HEADS * GDN_HEAD_DIM
    gdn_dt = jnp.exp(unif((DEPTH, GDN_HEADS), math.log(1e-3), math.log(1e-1)))
    page_table = jax.random.permutation(next(keys), n_pool)[:DEC_BATCH * n_pages]
    page_table = page_table.reshape(DEC_BATCH, n_pages).astype(jnp.int32)
    s5_n = jnp.pi * jnp.arange(S5_STATE, dtype=f32)
    return {
        'x_prompt': nrm((BATCH, SEQ, D_MODEL)),
        'x_sample': nrm((DEC_BATCH, DEC_SEQ, D_MODEL)),
        'cache_diff_k': nrm((DEPTH, n_pool, PAGE_SIZE, 2, DIFF_HEAD_DIM)),
        'cache_diff_v': nrm((DEPTH, n_pool, PAGE_SIZE, 2 * DIFF_HEAD_DIM)),
        'cache_mla_ckv': nrm((DEPTH, n_pool, PAGE_SIZE, MLA_KV_LORA)),
        'cache_mla_krope': nrm((DEPTH, n_pool, PAGE_SIZE, MLA_ROPE)),
        'state_gdn': nrm((DEPTH, DEC_BATCH, GDN_HEADS, GDN_HEAD_DIM, GDN_HEAD_DIM), 0.1),
        'state_gdn_conv': nrm((DEPTH, DEC_BATCH, GDN_CONV - 1, qkv_width)),
        'state_s5_re': nrm((DEPTH, DEC_BATCH, S5_GROUPS, S5_STATE), 0.5),
        'state_s5_im': nrm((DEPTH, DEC_BATCH, S5_GROUPS, S5_STATE), 0.5),
        'state_ffn_conv': nrm((DEPTH, DEC_BATCH, FFN_CONV - 1, 2 * D_FF)),
        'page_table': page_table,
        'c_prompt': nrm((BATCH, D_MODEL)),
        'c_sample': nrm((DEC_BATCH, D_MODEL)),
        'w_ada': nrm((DEPTH, D_MODEL, 6 * D_MODEL), 0.5 * D_MODEL ** -0.5),
        'b_ada': nrm((DEPTH, 6 * D_MODEL), 0.02),
        'norm_mix': gain((DEPTH, D_MODEL)),
        'norm_ffn': gain((DEPTH, D_MODEL)),
        'w_in': nrm((DEPTH, D_MODEL, d_in), D_MODEL ** -0.5),
        'gdn_conv_w': nrm((DEPTH, GDN_CONV, qkv_width), 0.5),
        'gdn_a_log': jnp.log(unif((DEPTH, GDN_HEADS), 1.0, 16.0)),
        'gdn_dt_bias': gdn_dt + jnp.log(-jnp.expm1(-gdn_dt)),
        'gdn_norm': gain((DEPTH, GDN_HEAD_DIM)),
        'diff_q_norm': gain((DEPTH, DIFF_HEAD_DIM)),
        'diff_k_norm': gain((DEPTH, DIFF_HEAD_DIM)),
        'diff_lq1': nrm((DEPTH, DIFF_HEAD_DIM), 0.1),
        'diff_lk1': nrm((DEPTH, DIFF_HEAD_DIM), 0.1),
        'diff_lq2': nrm((DEPTH, DIFF_HEAD_DIM), 0.1),
        'diff_lk2': nrm((DEPTH, DIFF_HEAD_DIM), 0.1),
        'diff_subln': gain((DEPTH, 2 * DIFF_HEAD_DIM)),
        's5_a_re': -0.5 + nrm((DEPTH, S5_GROUPS, S5_STATE), 0.01),
        's5_a_im': s5_n + nrm((DEPTH, S5_GROUPS, S5_STATE), 0.01),
        's5_log_dt': unif((DEPTH, S5_GROUPS), math.log(1e-3), math.log(1e-1)),
        's5_b_re': nrm((DEPTH, S5_GROUPS, S5_STATE, S5_GROUP_CH), (2 * S5_GROUP_CH) ** -0.5),
        's5_b_im': nrm((DEPTH, S5_GROUPS, S5_STATE, S5_GROUP_CH), (2 * S5_GROUP_CH) ** -0.5),
        's5_c_re': nrm((DEPTH, S5_GROUPS, S5_GROUP_CH, S5_STATE), (2 * S5_STATE) ** -0.5),
        's5_c_im': nrm((DEPTH, S5_GROUPS, S5_GROUP_CH, S5_STATE), (2 * S5_STATE) ** -0.5),
        's5_d': nrm((DEPTH, BRANCH_WIDTH), 0.5),
        's5_w_glu': nrm((DEPTH, BRANCH_WIDTH, 2 * BRANCH_WIDTH), BRANCH_WIDTH ** -0.5),
        'mla_q_a_norm': gain((DEPTH, MLA_Q_LORA)),
        'mla_w_qb': nrm((DEPTH, MLA_Q_LORA, MLA_HEADS * (MLA_NOPE + MLA_ROPE)), MLA_Q_LORA ** -0.5),
        'mla_q_norm': gain((DEPTH, MLA_NOPE + MLA_ROPE)),
        'mla_kv_norm': gain((DEPTH, MLA_KV_LORA)),
        'mla_k_norm': gain((DEPTH, MLA_ROPE)),
        'mla_w_kvb': nrm((DEPTH, MLA_KV_LORA, MLA_HEADS, MLA_NOPE + MLA_V), MLA_KV_LORA ** -0.5),
        'w_branch': nrm((DEPTH, N_BRANCH, BRANCH_WIDTH, D_MODEL), BRANCH_WIDTH ** -0.5),
        'w_out': nrm((DEPTH, D_MODEL, D_MODEL), D_MODEL ** -0.5),
        'ffn_w_up': nrm((DEPTH, D_MODEL, 2 * D_FF), D_MODEL ** -0.5),
        'ffn_conv_w': nrm((DEPTH, FFN_CONV, 2 * D_FF), 0.5),
        'ffn_w_down': nrm((DEPTH, D_FF, D_MODEL), D_FF ** -0.5),
    }


def reference(x_prompt, x_sample, cache_diff_k, cache_diff_v, cache_mla_ckv, cache_mla_krope,
              state_gdn, state_gdn_conv, state_s5_re, state_s5_im, state_ffn_conv, page_table,
              c_prompt, c_sample, w_ada, b_ada, norm_mix, norm_ffn, w_in,
              gdn_conv_w, gdn_a_log, gdn_dt_bias, gdn_norm,
              diff_q_norm, diff_k_norm, diff_lq1, diff_lk1, diff_lq2, diff_lk2, diff_subln,
              s5_a_re, s5_a_im, s5_log_dt, s5_b_re, s5_b_im, s5_c_re, s5_c_im, s5_d, s5_w_glu,
              mla_q_a_norm, mla_w_qb, mla_q_norm, mla_kv_norm, mla_k_norm, mla_w_kvb,
              w_branch, w_out, ffn_w_up, ffn_conv_w, ffn_w_down):
    weights = dict(w_ada=w_ada, b_ada=b_ada, norm_mix=norm_mix, norm_ffn=norm_ffn, w_in=w_in,
                   gdn_conv_w=gdn_conv_w, gdn_a_log=gdn_a_log, gdn_dt_bias=gdn_dt_bias, gdn_norm=gdn_norm,
                   diff_q_norm=diff_q_norm, diff_k_norm=diff_k_norm, diff_lq1=diff_lq1, diff_lk1=diff_lk1,
                   diff_lq2=diff_lq2, diff_lk2=diff_lk2, diff_subln=diff_subln,
                   s5_a_re=s5_a_re, s5_a_im=s5_a_im, s5_log_dt=s5_log_dt, s5_b_re=s5_b_re, s5_b_im=s5_b_im,
                   s5_c_re=s5_c_re, s5_c_im=s5_c_im, s5_d=s5_d, s5_w_glu=s5_w_glu,
                   mla_q_a_norm=mla_q_a_norm, mla_w_qb=mla_w_qb, mla_q_norm=mla_q_norm,
                   mla_kv_norm=mla_kv_norm, mla_k_norm=mla_k_norm, mla_w_kvb=mla_w_kvb,
                   w_branch=w_branch, w_out=w_out, ffn_w_up=ffn_w_up, ffn_conv_w=ffn_conv_w, ffn_w_down=ffn_w_down)
    past_len = page_table.shape[1] * PAGE_SIZE
    pos_p = jnp.arange(x_prompt.shape[1], dtype=jnp.int32)
    pos_s = past_len + jnp.arange(x_sample.shape[1], dtype=jnp.int32)
    xp, xs = x_prompt, x_sample
    rows_p = {name: [] for name in STATE_NAMES}
    rows_s = {name: [] for name in STATE_NAMES}
    for i in range(DEPTH):
        p = {name: w[i] for name, w in weights.items()}
        xp, new_p = _layer(i, xp, c_prompt, pos_p, _empty_state(x_prompt.shape[0], x_prompt.dtype), p, _prompt_attend)
        st_s = dict(gdn=state_gdn[i], gdn_conv=state_gdn_conv[i], s5_re=state_s5_re[i],
                    s5_im=state_s5_im[i], ffn_conv=state_ffn_conv[i])
        attend_s = _make_paged_attend(i, pos_s, page_table, cache_diff_k, cache_diff_v, cache_mla_ckv, cache_mla_krope)
        xs, new_s = _layer(i, xs, c_sample, pos_s, st_s, p, attend_s)
        for name in STATE_NAMES:
            rows_p[name].append(new_p[name])
            rows_s[name].append(new_s[name])
    out_p = {name: jnp.stack(v, axis=0) for name, v in rows_p.items()}
    out_s = {name: jnp.stack(v, axis=0) for name, v in rows_s.items()}
    return (xp, xs,
            out_p['diff_k'], out_p['diff_v'], out_p['mla_ckv'], out_p['mla_krope'], out_p['gdn'],
            out_p['gdn_conv'], out_p['s5_re'], out_p['s5_im'], out_p['ffn_conv'],
            out_s['diff_k'], out_s['diff_v'], out_s['mla_ckv'], out_s['mla_krope'], out_s['gdn'],
            out_s['gdn_conv'], out_s['s5_re'], out_s['s5_im'], out_s['ffn_conv'])
```

```python
import functools
import math

import jax
import jax.numpy as jnp
from jax import lax
from jax.experimental import pallas as pl
from jax.experimental.pallas import tpu as pltpu

D_MODEL = 2048
DEPTH = 2
PAGE_SIZE = 128
N_BRANCH = 4
BRANCH_WIDTH = D_MODEL // N_BRANCH
GDN_HEADS = 4
GDN_HEAD_DIM = BRANCH_WIDTH // GDN_HEADS
GDN_CONV = 4
GDN_CHUNK = 64
DIFF_HEADS = 4
DIFF_HEAD_DIM = BRANCH_WIDTH // (2 * DIFF_HEADS)
S5_GROUP_CH = 16
S5_GROUPS = BRANCH_WIDTH // S5_GROUP_CH
S5_STATE = 64
MLA_HEADS = 4
MLA_NOPE = 64
MLA_ROPE = 32
MLA_V = BRANCH_WIDTH // MLA_HEADS
MLA_Q_LORA = 384
MLA_KV_LORA = 256
ROPE_THETA = 10000.0
D_FF = 5632
FFN_CONV = 3
NEG_INF = -1e30
EPS = 1e-6

QKV_WIDTH = 3 * GDN_HEADS * GDN_HEAD_DIM
OFF_QKV = 0
OFF_Z = OFF_QKV + QKV_WIDTH
OFF_DQ = OFF_Z + BRANCH_WIDTH
OFF_DK = OFF_DQ + BRANCH_WIDTH
OFF_DV = OFF_DK + 2 * DIFF_HEAD_DIM
OFF_SU = OFF_DV + 2 * DIFF_HEAD_DIM
OFF_MQA = OFF_SU + BRANCH_WIDTH
OFF_MKVA = OFF_MQA + MLA_Q_LORA
OFF_MKR = OFF_MKVA + MLA_KV_LORA
OFF_GB = OFF_MKR + MLA_ROPE
OFF_GA = OFF_GB + GDN_HEADS
OFF_GATES = 4096
IN_WIDTH = OFF_GATES + N_BRANCH * D_MODEL

LANES = 128
VMEM_LIMIT = 56 * 1024 * 1024
PAGES_PER_STEP = 32

F32 = jnp.float32
BF16 = jnp.bfloat16


def _mm_kernel(a_ref, b_ref, o_ref):
    o_ref[...] = jnp.dot(a_ref[...], b_ref[...], preferred_element_type=F32).astype(o_ref.dtype)


def _mm(a, b, *, tm=1024, tn=512, out_dtype=F32):
    m, k = a.shape
    n = b.shape[1]
    tm = min(tm, m)
    tn = min(tn, n)
    assert m % tm == 0 and n % tn == 0, (m, n, tm, tn)
    return pl.pallas_call(
        _mm_kernel,
        grid=(m // tm, n // tn),
        in_specs=[pl.BlockSpec((tm, k), lambda i, j: (i, 0)),
                  pl.BlockSpec((k, tn), lambda i, j: (0, j))],
        out_specs=pl.BlockSpec((tm, tn), lambda i, j: (i, j)),
        out_shape=jax.ShapeDtypeStruct((m, n), out_dtype),
        compiler_params=pltpu.CompilerParams(
            dimension_semantics=("parallel", "parallel"), vmem_limit_bytes=VMEM_LIMIT),
    )(a.astype(BF16), b.astype(BF16))


def _softmax_update(s, v, m_sc, l_sc, acc_sc):
    m_old = m_sc[...]
    m_new = jnp.maximum(m_old, jnp.max(s, axis=-1, keepdims=True))
    p = jnp.exp(s - m_new)
    corr = jnp.exp(m_old - m_new)
    l_sc[...] = l_sc[...] * corr + jnp.sum(p, axis=-1, keepdims=True)
    acc_sc[...] = acc_sc[...] * corr + jnp.dot(p.astype(BF16), v, preferred_element_type=F32)
    m_sc[...] = m_new


def _prompt_attn_kernel(q_ref, k_ref, v_ref, o_ref, m_sc, l_sc, acc_sc, *, n_heads, tq, tk, slopes):
    qi = pl.program_id(2)
    rows = n_heads * tq
    q = q_ref[0, 0].reshape(rows, q_ref.shape[-1])
    m_sc[...] = jnp.full(m_sc.shape, NEG_INF, F32)
    l_sc[...] = jnp.zeros(l_sc.shape, F32)
    acc_sc[...] = jnp.zeros(acc_sc.shape, F32)
    row = lax.broadcasted_iota(jnp.int32, (rows, 1), 0)
    q_pos = qi * tq + row % tq
    if slopes is not None:
        slope = jnp.concatenate([jnp.full((tq, 1), sl, F32) for sl in slopes], axis=0)

    def body(kb, carry):
        start = pl.multiple_of(kb * tk, tk)
        k = k_ref[0, 0, pl.ds(start, tk), :]
        v = v_ref[0, pl.ds(start, tk), :]
        s = lax.dot_general(q, k, (((1,), (1,)), ((), ())), preferred_element_type=F32)
        k_pos = start + lax.broadcasted_iota(jnp.int32, (1, tk), 1)
        rel = q_pos - k_pos
        if slopes is not None:
            s = s - slope * rel.astype(F32)
        s = jnp.where(rel >= 0, s, NEG_INF)
        _softmax_update(s, v, m_sc, l_sc, acc_sc)
        return carry

    n_blocks = (qi * tq + tq + tk - 1) // tk
    lax.fori_loop(0, n_blocks, body, 0)
    out = acc_sc[...] / l_sc[...]
    o_ref[0, 0] = out.reshape(n_heads, tq, out.shape[-1])


def _prompt_attention(q, k, v, slopes, *, tq=256, tk=256):
    b, n_maps, n_heads, length, dk = q.shape
    dv = v.shape[-1]
    rows = n_heads * tq
    kern = functools.partial(_prompt_attn_kernel, n_heads=n_heads, tq=tq, tk=tk, slopes=slopes)
    return pl.pallas_call(
        kern,
        grid=(b, n_maps, length // tq),
        in_specs=[pl.BlockSpec((1, 1, n_heads, tq, dk), lambda bi, mi, qi: (bi, mi, 0, qi, 0)),
                  pl.BlockSpec((1, 1, length, dk), lambda bi, mi, qi: (bi, mi, 0, 0)),
                  pl.BlockSpec((1, length, dv), lambda bi, mi, qi: (bi, 0, 0))],
        out_specs=pl.BlockSpec((1, 1, n_heads, tq, dv), lambda bi, mi, qi: (bi, mi, 0, qi, 0)),
        out_shape=jax.ShapeDtypeStruct((b, n_maps, n_heads, length, dv), F32),
        scratch_shapes=[pltpu.VMEM((rows, 1), F32), pltpu.VMEM((rows, 1), F32), pltpu.VMEM((rows, dv), F32)],
        compiler_params=pltpu.CompilerParams(
            dimension_semantics=("parallel", "parallel", "parallel"), vmem_limit_bytes=VMEM_LIMIT),
    )(q.astype(BF16), k.astype(BF16), v.astype(BF16))


def _paged_copies(pt_ref, bi, ci, slot, layer, pc, specs, sems):
    copies = []
    for j in range(pc):
        page = pt_ref[bi, ci * pc + j]
        for a, (hbm, buf, keys_on_lanes) in enumerate(specs):
            src = hbm.at[layer, page]
            if keys_on_lanes:
                dst = buf.at[slot, pl.ds(0, src.shape[0]), pl.ds(j * PAGE_SIZE, PAGE_SIZE)]
            else:
                dst = buf.at[slot, pl.ds(j * PAGE_SIZE, PAGE_SIZE), :]
            copies.append(pltpu.make_async_copy(src, dst, sems.at[slot, a]))
    return copies


def _paged_pipeline(pt_ref, layer, pc, specs, sems):
    bi, ci = pl.program_id(0), pl.program_id(1)
    nb, nc = pl.num_programs(0), pl.num_programs(1)
    step = bi * nc + ci
    slot = step % 2

    @pl.when(step == 0)
    def _():
        for cp in _paged_copies(pt_ref, bi, ci, slot, layer, pc, specs, sems):
            cp.start()

    @pl.when(step + 1 < nb * nc)
    def _():
        nxt = step + 1
        for cp in _paged_copies(pt_ref, nxt // nc, nxt % nc, 1 - slot, layer, pc, specs, sems):
            cp.start()

    for cp in _paged_copies(pt_ref, bi, ci, slot, layer, pc, specs, sems):
        cp.wait()
    return slot


def _paged_diff_kernel(pt_ref, q_ref, slope_ref, trow_ref, knew_ref, vnew_ref, kc_hbm, vc_hbm, o_ref,
                       kbuf, vbuf, sems, m_sc, l_sc, acc_sc, *, layer, pc, past_len):
    ci = pl.program_id(1)
    slot = _paged_pipeline(pt_ref, layer, pc, [(kc_hbm, kbuf, True), (vc_hbm, vbuf, False)], sems)

    @pl.when(ci == 0)
    def _():
        m_sc[...] = jnp.full(m_sc.shape, NEG_INF, F32)
        l_sc[...] = jnp.zeros(l_sc.shape, F32)
        acc_sc[...] = jnp.zeros(acc_sc.shape, F32)

    q = q_ref[0]
    slope = slope_ref[...]
    t_row = trow_ref[...]
    span = pc * PAGE_SIZE
    s = jnp.dot(q, kbuf[slot].astype(BF16), preferred_element_type=F32)
    k_off = (ci * span + lax.broadcasted_iota(jnp.int32, (1, span), 1)).astype(F32)
    rel = (t_row + float(past_len)) - k_off
    s = s - slope * rel
    _softmax_update(s, vbuf[slot].astype(BF16), m_sc, l_sc, acc_sc)

    @pl.when(ci == pl.num_programs(1) - 1)
    def _():
        s_new = jnp.dot(q, knew_ref[0], preferred_element_type=F32)
        rel_new = t_row - lax.broadcasted_iota(jnp.int32, (1, PAGE_SIZE), 1).astype(F32)
        s_new = jnp.where(rel_new >= 0, s_new - slope * rel_new, NEG_INF)
        _softmax_update(s_new, vnew_ref[0], m_sc, l_sc, acc_sc)
        o_ref[0] = acc_sc[...] / l_sc[...]


def _paged_diff_attention(layer, q, slope, t_row, k_new_t, v_new, cache_k_t, cache_v, page_table, past_len):
    b, rows, feat = q.shape
    dv = cache_v.shape[-1]
    pc = PAGES_PER_STEP
    n_chunks = page_table.shape[1] // pc
    span = pc * PAGE_SIZE
    kern = functools.partial(_paged_diff_kernel, layer=layer, pc=pc, past_len=past_len)
    grid_spec = pltpu.PrefetchScalarGridSpec(
        num_scalar_prefetch=1,
        grid=(b, n_chunks),
        in_specs=[pl.BlockSpec((1, rows, feat), lambda bi, ci, pt: (bi, 0, 0)),
                  pl.BlockSpec((rows, 1), lambda bi, ci, pt: (0, 0)),
                  pl.BlockSpec((rows, 1), lambda bi, ci, pt: (0, 0)),
                  pl.BlockSpec((1, feat, PAGE_SIZE), lambda bi, ci, pt: (bi, 0, 0)),
                  pl.BlockSpec((1, PAGE_SIZE, dv), lambda bi, ci, pt: (bi, 0, 0)),
                  pl.BlockSpec(memory_space=pl.ANY),
                  pl.BlockSpec(memory_space=pl.ANY)],
        out_specs=pl.BlockSpec((1, rows, dv), lambda bi, ci, pt: (bi, 0, 0)),
        scratch_shapes=[pltpu.VMEM((2, feat, span), F32), pltpu.VMEM((2, span, dv), F32),
                        pltpu.SemaphoreType.DMA((2, 2)),
                        pltpu.VMEM((rows, 1), F32), pltpu.VMEM((rows, 1), F32), pltpu.VMEM((rows, dv), F32)])
    return pl.pallas_call(
        kern, grid_spec=grid_spec,
        out_shape=jax.ShapeDtypeStruct((b, rows, dv), F32),
        compiler_params=pltpu.CompilerParams(
            dimension_semantics=("arbitrary", "arbitrary"), vmem_limit_bytes=VMEM_LIMIT),
    )(page_table, q, slope, t_row, k_new_t, v_new, cache_k_t, cache_v)


def _paged_mla_kernel(pt_ref, ql_ref, qr_ref, trow_ref, cnew_ref, rnew_ref, cc_hbm, rc_hbm, o_ref,
                      cbuf, rbuf, sems, m_sc, l_sc, acc_sc, *, layer, pc):
    bi, ci = pl.program_id(0), pl.program_id(1)

    @pl.when((bi == 0) & (ci == 0))
    def _():
        rbuf[...] = jnp.zeros(rbuf.shape, F32)

    slot = _paged_pipeline(pt_ref, layer, pc, [(cc_hbm, cbuf, False), (rc_hbm, rbuf, True)], sems)

    @pl.when(ci == 0)
    def _():
        m_sc[...] = jnp.full(m_sc.shape, NEG_INF, F32)
        l_sc[...] = jnp.zeros(l_sc.shape, F32)
        acc_sc[...] = jnp.zeros(acc_sc.shape, F32)

    ql = ql_ref[0]
    qr = qr_ref[0]
    ckv = cbuf[slot].astype(BF16)
    s = lax.dot_general(ql, ckv, (((1,), (1,)), ((), ())), preferred_element_type=F32)
    s = s + jnp.dot(qr, rbuf[slot].astype(BF16), preferred_element_type=F32)
    _softmax_update(s, ckv, m_sc, l_sc, acc_sc)

    @pl.when(ci == pl.num_programs(1) - 1)
    def _():
        c_new = cnew_ref[0]
        s_new = lax.dot_general(ql, c_new, (((1,), (1,)), ((), ())), preferred_element_type=F32)
        s_new = s_new + jnp.dot(qr, rnew_ref[0], preferred_element_type=F32)
        rel_new = trow_ref[...] - lax.broadcasted_iota(jnp.int32, (1, PAGE_SIZE), 1).astype(F32)
        s_new = jnp.where(rel_new >= 0, s_new, NEG_INF)
        _softmax_update(s_new, c_new, m_sc, l_sc, acc_sc)
        o_ref[0] = acc_sc[...] / l_sc[...]


def _paged_mla_attention(layer, q_lat, q_rope, t_row, c_new, r_new_t, cache_ckv, cache_krope_t, page_table):
    b, rows, lat = q_lat.shape
    pc = PAGES_PER_STEP
    n_chunks = page_table.shape[1] // pc
    span = pc * PAGE_SIZE
    kern = functools.partial(_paged_mla_kernel, layer=layer, pc=pc)
    grid_spec = pltpu.PrefetchScalarGridSpec(
        num_scalar_prefetch=1,
        grid=(b, n_chunks),
        in_specs=[pl.BlockSpec((1, rows, lat), lambda bi, ci, pt: (bi, 0, 0)),
                  pl.BlockSpec((1, rows, LANES), lambda bi, ci, pt: (bi, 0, 0)),
                  pl.BlockSpec((rows, 1), lambda bi, ci, pt: (0, 0)),
                  pl.BlockSpec((1, PAGE_SIZE, lat), lambda bi, ci, pt: (bi, 0, 0)),
                  pl.BlockSpec((1, LANES, PAGE_SIZE), lambda bi, ci, pt: (bi, 0, 0)),
                  pl.BlockSpec(memory_space=pl.ANY),
                  pl.BlockSpec(memory_space=pl.ANY)],
        out_specs=pl.BlockSpec((1, rows, lat), lambda bi, ci, pt: (bi, 0, 0)),
        scratch_shapes=[pltpu.VMEM((2, span, lat), F32), pltpu.VMEM((2, LANES, span), F32),
                        pltpu.SemaphoreType.DMA((2, 2)),
                        pltpu.VMEM((rows, 1), F32), pltpu.VMEM((rows, 1), F32), pltpu.VMEM((rows, lat), F32)])
    return pl.pallas_call(
        kern, grid_spec=grid_spec,
        out_shape=jax.ShapeDtypeStruct((b, rows, lat), F32),
        compiler_params=pltpu.CompilerParams(
            dimension_semantics=("arbitrary", "arbitrary"), vmem_limit_bytes=VMEM_LIMIT),
    )(page_table, q_lat, q_rope, t_row, c_new, r_new_t, cache_ckv, cache_krope_t)


def _rmsnorm(x, g):
    xf = x.astype(F32)
    y = xf * lax.rsqrt(jnp.mean(xf * xf, axis=-1, keepdims=True) + EPS)
    return (y * g.astype(F32)).astype(x.dtype)


def _l2norm(x):
    xf = x.astype(F32)
    return xf * lax.rsqrt(jnp.sum(xf * xf, axis=-1, keepdims=True) + EPS)


def _rope(x, pos):
    half = x.shape[-1] // 2
    inv = ROPE_THETA ** (-jnp.arange(half, dtype=F32) / half)
    ang = pos.astype(F32)[:, None] * inv
    ang = ang.reshape((pos.shape[0],) + (1,) * (x.ndim - 3) + (half,))
    cos, sin = jnp.cos(ang), jnp.sin(ang)
    xf = x.astype(F32)
    x1, x2 = xf[..., :half], xf[..., half:]
    return jnp.concatenate([x1 * cos - x2 * sin, x2 * cos + x1 * sin], axis=-1).astype(x.dtype)


def _alibi_slopes(n_heads):
    return [2.0 ** (-8.0 * (h + 1) / n_heads) for h in range(n_heads)]


def _causal_dwconv(x, buf, w):
    width, length = w.shape[0], x.shape[1]
    xx = jnp.concatenate([buf.astype(x.dtype), x], axis=1)
    y = w[0] * xx[:, :length]
    for j in range(1, width):
        y = y + w[j] * xx[:, j:j + length]
    return y, xx[:, length:]


def _gated_delta_rule(q, k, v, g, beta, s0):
    b, length, h, _ = q.shape
    c = math.gcd(length, GDN_CHUNK)
    n = length // c

    def chunks(t):
        t = t.reshape((b, n, c, h) + t.shape[3:])
        return jnp.moveaxis(t, (1, 3), (0, 2))

    qc, kc, vc, bc = chunks(q), chunks(k), chunks(v), chunks(beta)
    gc = jnp.cumsum(chunks(g), axis=-1)
    tril = jnp.tril(jnp.ones((c, c), dtype=bool))
    strict = jnp.tril(jnp.ones((c, c), dtype=bool), k=-1)
    diff = gc[..., :, None] - gc[..., None, :]
    decay = jnp.where(tril, jnp.exp(jnp.where(tril, diff, 0.0)), 0.0)
    k_beta = kc * bc[..., None]
    a = jnp.where(strict, jnp.einsum('nbhid,nbhjd->nbhij', k_beta, kc) * decay, 0.0)
    lmat = a + jnp.eye(c, dtype=F32)
    u = lax.linalg.triangular_solve(lmat, vc * bc[..., None], left_side=True, lower=True, unit_diagonal=True)
    w = lax.linalg.triangular_solve(lmat, k_beta * jnp.exp(gc)[..., None], left_side=True, lower=True,
                                    unit_diagonal=True)
    qk = jnp.where(tril, jnp.einsum('nbhid,nbhjd->nbhij', qc, kc) * decay, 0.0)

    def step(s, xs):
        q_i, k_i, u_i, w_i, g_i, qk_i = xs
        v_new = u_i - jnp.einsum('bhcd,bhdv->bhcv', w_i, s)
        o = (jnp.einsum('bhcd,bhdv->bhcv', q_i * jnp.exp(g_i)[..., None], s)
             + jnp.einsum('bhij,bhjv->bhiv', qk_i, v_new))
        g_last = g_i[..., -1]
        k_dec = k_i * jnp.exp(g_last[..., None] - g_i)[..., None]
        s = s * jnp.exp(g_last)[..., None, None] + jnp.einsum('bhcd,bhcv->bhdv', k_dec, v_new)
        return s, o

    s_final, o = lax.scan(step, s0, (qc, kc, u, w, gc, qk))
    o = jnp.moveaxis(o, (0, 2), (1, 3)).reshape(b, length, h, -1)
    return o, s_final


def _linear_combine(e1, e2):
    a1, b1 = e1
    a2, b2 = e2
    return a1 * a2, a2 * b1 + b2


def _s5(u, x0_re, x0_im, p):
    b, length, _ = u.shape
    uf = u.astype(F32).reshape(b, length, S5_GROUPS, S5_GROUP_CH)
    lam = lax.complex(p['s5_a_re'].astype(F32), p['s5_a_im'].astype(F32))
    dt = jnp.exp(p['s5_log_dt'].astype(F32))[:, None]
    lam_bar = jnp.exp(lam * dt)
    b_bar = ((lam_bar - 1.0) / lam)[..., None] * lax.complex(p['s5_b_re'].astype(F32), p['s5_b_im'].astype(F32))
    c_mat = lax.complex(p['s5_c_re'].astype(F32), p['s5_c_im'].astype(F32))
    bu = jnp.einsum('gpc,blgc->blgp', b_bar, uf.astype(jnp.complex64))
    x0 = lax.complex(x0_re.astype(F32), x0_im.astype(F32))
    bu = bu.at[:, 0].add(lam_bar * x0)
    a = jnp.broadcast_to(lam_bar, bu.shape)
    _, states = lax.associative_scan(_linear_combine, (a, bu), axis=1)
    y = (jnp.einsum('gcp,blgp->blgc', c_mat, states).real
         + p['s5_d'].astype(F32).reshape(S5_GROUPS, S5_GROUP_CH) * uf)
    v = jax.nn.gelu(y.reshape(b, length, BRANCH_WIDTH)).astype(u.dtype)
    glu = _mm(v.reshape(b * length, BRANCH_WIDTH), p['s5_w_glu_bf']).reshape(b, length, 2 * BRANCH_WIDTH)
    glu_a, glu_b = jnp.split(glu, 2, axis=-1)
    last = states[:, -1]
    return glu_a * jax.nn.sigmoid(glu_b), last.real.astype(x0_re.dtype), last.imag.astype(x0_im.dtype)


def _attend_prompt_diff(dq, dk, d_v):
    q = (dq * DIFF_HEAD_DIM ** -0.5).transpose(0, 2, 3, 1, 4)
    k = dk.transpose(0, 2, 1, 3)
    out = _prompt_attention(q, k, d_v, _alibi_slopes(DIFF_HEADS))
    return out.transpose(0, 3, 1, 2, 4)


def _attend_prompt_mla(q_cat, k_cat, ckv):
    pad = 3 * LANES - q_cat.shape[-1]
    q = (q_cat * (MLA_NOPE + MLA_ROPE) ** -0.5).transpose(0, 2, 1, 3)[:, None]
    q = jnp.pad(q, ((0, 0),) * 4 + ((0, pad),))
    k = jnp.pad(k_cat, ((0, 0), (0, 0), (0, pad)))[:, None]
    out = _prompt_attention(q, k, ckv, None)
    return out[:, 0].transpose(0, 2, 1, 3)


def _pad_rows(x, n, axis):
    pad = [(0, 0)] * x.ndim
    pad[axis] = (0, n - x.shape[axis])
    return jnp.pad(x, pad)


def _attend_paged_diff(layer, dq, dk, d_v, cache_k_t, cache_v, page_table):
    b, t = dq.shape[:2]
    d = DIFF_HEAD_DIM
    past_len = page_table.shape[1] * PAGE_SIZE
    qs = (dq * d ** -0.5).transpose(0, 2, 1, 3, 4).reshape(b, 2, t * DIFF_HEADS, d)
    zeros = jnp.zeros_like(qs[:, 0])
    q = jnp.concatenate([jnp.concatenate([qs[:, 0], zeros], axis=-1),
                         jnp.concatenate([zeros, qs[:, 1]], axis=-1)], axis=1).astype(BF16)
    slope = jnp.tile(jnp.asarray(_alibi_slopes(DIFF_HEADS), F32), 2 * t)[:, None]
    t_row = jnp.tile(jnp.repeat(jnp.arange(t, dtype=F32), DIFF_HEADS), 2)[:, None]
    k_new_t = _pad_rows(dk.reshape(b, t, 2 * d), PAGE_SIZE, 1).transpose(0, 2, 1).astype(BF16)
    v_new = _pad_rows(d_v, PAGE_SIZE, 1).astype(BF16)
    out = _paged_diff_attention(layer, q, slope, t_row, k_new_t, v_new, cache_k_t, cache_v, page_table, past_len)
    return out.reshape(b, 2, t, DIFF_HEADS, 2 * d).transpose(0, 2, 1, 3, 4)


def _attend_paged_mla(layer, q_lat, q_rope, ckv, krope, cache_ckv, cache_krope_t, page_table):
    b, t = q_lat.shape[:2]
    scale = (MLA_NOPE + MLA_ROPE) ** -0.5
    ql = (q_lat * scale).reshape(b, t * MLA_HEADS, MLA_KV_LORA).astype(BF16)
    qr = _pad_rows((q_rope * scale).reshape(b, t * MLA_HEADS, MLA_ROPE), LANES, 2).astype(BF16)
    t_row = jnp.repeat(jnp.arange(t, dtype=F32), MLA_HEADS)[:, None]
    c_new = _pad_rows(ckv, PAGE_SIZE, 1).astype(BF16)
    r_new_t = _pad_rows(_pad_rows(krope, PAGE_SIZE, 1), LANES, 2).transpose(0, 2, 1).astype(BF16)
    out = _paged_mla_attention(layer, ql, qr, t_row, c_new, r_new_t, cache_ckv, cache_krope_t, page_table)
    return out.reshape(b, t, MLA_HEADS, MLA_KV_LORA)


def _layer(i, x, mod, pos, st, p, paged):
    b, length, _ = x.shape
    tokens = b * length
    sh_m, sc_m, gt_m, sh_f, sc_f, gt_f = jnp.split(mod[:, None, :], 6, axis=-1)
    h = _rmsnorm(x, p['norm_mix']) * (1 + sc_m) + sh_m
    proj = _mm(h.reshape(tokens, D_MODEL), p['w_in_bf']).reshape(b, length, IN_WIDTH)
    g_qkv = proj[..., OFF_QKV:OFF_Z]
    g_z = proj[..., OFF_Z:OFF_DQ]
    d_q = proj[..., OFF_DQ:OFF_DK]
    d_k = proj[..., OFF_DK:OFF_DV]
    d_v = proj[..., OFF_DV:OFF_SU]
    s_u = proj[..., OFF_SU:OFF_MQA]
    m_qa = proj[..., OFF_MQA:OFF_MKVA]
    m_kva = proj[..., OFF_MKVA:OFF_MKR]
    m_kr = proj[..., OFF_MKR:OFF_GB]
    g_b = proj[..., OFF_GB:OFF_GA]
    g_a = proj[..., OFF_GA:OFF_GA + GDN_HEADS]
    gates = proj[..., OFF_GATES:]

    qkv, gdn_conv_new = _causal_dwconv(g_qkv, st['gdn_conv'], p['gdn_conv_w'])
    qkv = jax.nn.silu(qkv).reshape(b, length, 3, GDN_HEADS, GDN_HEAD_DIM)
    gq = _l2norm(qkv[:, :, 0]) * GDN_HEAD_DIM ** -0.5
    gk = _l2norm(qkv[:, :, 1])
    gv = qkv[:, :, 2].astype(F32)
    beta = jax.nn.sigmoid(g_b.astype(F32))
    log_decay = -jnp.exp(p['gdn_a_log'].astype(F32)) * jax.nn.softplus(g_a.astype(F32) + p['gdn_dt_bias'].astype(F32))
    o_a, gdn_state_new = _gated_delta_rule(gq, gk, gv, log_decay, beta, st['gdn'].astype(F32))
    o_a = _rmsnorm(o_a, p['gdn_norm']) * jax.nn.silu(g_z.reshape(b, length, GDN_HEADS, GDN_HEAD_DIM).astype(F32))
    y_a = o_a.reshape(b, length, BRANCH_WIDTH)

    dq = _rmsnorm(d_q.reshape(b, length, DIFF_HEADS, 2, DIFF_HEAD_DIM).swapaxes(2, 3), p['diff_q_norm'])
    dk = _rmsnorm(d_k.reshape(b, length, 2, DIFF_HEAD_DIM), p['diff_k_norm'])
    lam_init = 0.8 - 0.6 * math.exp(-0.3 * i)
    lam = (jnp.exp(jnp.sum(p['diff_lq1'].astype(F32) * p['diff_lk1'].astype(F32)))
           - jnp.exp(jnp.sum(p['diff_lq2'].astype(F32) * p['diff_lk2'].astype(F32))) + lam_init)
    if paged is None:
        att_b = _attend_prompt_diff(dq, dk, d_v)
    else:
        att_b = _attend_paged_diff(i, dq, dk, d_v, paged['diff_k_t'], paged['diff_v'], paged['page_table'])
    o_b = att_b[:, :, 0] - lam * att_b[:, :, 1]
    o_b = _rmsnorm(o_b, p['diff_subln']) * (1.0 - lam_init)
    y_b = o_b.reshape(b, length, BRANCH_WIDTH)

    y_c, s5_re_new, s5_im_new = _s5(s_u, st['s5_re'], st['s5_im'], p)

    mq = _mm(_rmsnorm(m_qa, p['mla_q_a_norm']).reshape(tokens, MLA_Q_LORA), p['mla_w_qb_bf'])
    mq = _rmsnorm(mq.reshape(b, length, MLA_HEADS, MLA_NOPE + MLA_ROPE), p['mla_q_norm'])
    q_rope = _rope(mq[..., MLA_NOPE:], pos)
    q_lat = jnp.einsum('blhd,rhd->blhr', mq[..., :MLA_NOPE], p['mla_w_kvb'][..., :MLA_NOPE])
    ckv = _rmsnorm(m_kva, p['mla_kv_norm'])
    krope = _rope(_rmsnorm(m_kr, p['mla_k_norm']), pos)
    if paged is None:
        q_cat = jnp.concatenate([q_lat, q_rope], axis=-1)
        k_cat = jnp.concatenate([ckv, krope], axis=-1)
        att_d = _attend_prompt_mla(q_cat, k_cat, ckv)
    else:
        att_d = _attend_paged_mla(i, q_lat, q_rope, ckv, krope, paged['mla_ckv'], paged['mla_krope_t'],
                                  paged['page_table'])
    y_d = jnp.einsum('blhr,rhv->blhv', att_d, p['mla_w_kvb'][..., MLA_NOPE:]).reshape(b, length, BRANCH_WIDTH)

    gate = jax.nn.sigmoid(gates.reshape(b, length, N_BRANCH, D_MODEL))
    merged = 0.0
    for n, y in enumerate((y_a, y_b, y_c, y_d)):
        z = _mm(y.reshape(tokens, BRANCH_WIDTH), p['w_branch_bf'][n]).reshape(b, length, D_MODEL)
        merged = merged + gate[:, :, n] * z
    x = x + gt_m * _mm(merged.reshape(tokens, D_MODEL), p['w_out_bf']).reshape(b, length, D_MODEL)

    hf = _rmsnorm(x, p['norm_ffn']) * (1 + sc_f) + sh_f
    up = _mm(hf.reshape(tokens, D_MODEL), p['ffn_w_up_bf']).reshape(b, length, 2 * D_FF)
    up, ffn_conv_new = _causal_dwconv(up, st['ffn_conv'], p['ffn_conv_w'])
    u_gate, u_val = jnp.split(up, 2, axis=-1)
    act = (jax.nn.silu(u_gate) * u_val).reshape(tokens, D_FF)
    x = x + gt_f * _mm(act, p['ffn_w_down_bf'], tm=512).reshape(b, length, D_MODEL)

    new = dict(diff_k=dk, diff_v=d_v, mla_ckv=ckv, mla_krope=krope,
               gdn=gdn_state_new.astype(st['gdn'].dtype), gdn_conv=gdn_conv_new,
               s5_re=s5_re_new, s5_im=s5_im_new, ffn_conv=ffn_conv_new)
    return x, new


STATE_NAMES = ('diff_k', 'diff_v', 'mla_ckv', 'mla_krope', 'gdn', 'gdn_conv', 's5_re', 's5_im', 'ffn_conv')


def _reorder_w_in(w):
    pad = jnp.zeros((w.shape[0], OFF_GATES - (OFF_GA + GDN_HEADS)), w.dtype)
    return jnp.concatenate([w[:, :2048], w[:, 2056:3976], w[:, 3976:4008], w[:, 2048:2056], pad, w[:, 4008:]],
                           axis=1)


def kernel(x_prompt, x_sample, cache_diff_k, cache_diff_v, cache_mla_ckv, cache_mla_krope, state_gdn, state_gdn_conv, state_s5_re, state_s5_im, state_ffn_conv, page_table, c_prompt, c_sample, w_ada, b_ada, norm_mix, norm_ffn, w_in, gdn_conv_w, gdn_a_log, gdn_dt_bias, gdn_norm, diff_q_norm, diff_k_norm, diff_lq1, diff_lk1, diff_lq2, diff_lk2, diff_subln, s5_a_re, s5_a_im, s5_log_dt, s5_b_re, s5_b_im, s5_c_re, s5_c_im, s5_d, s5_w_glu, mla_q_a_norm, mla_w_qb, mla_q_norm, mla_kv_norm, mla_k_norm, mla_w_kvb, w_branch, w_out, ffn_w_up, ffn_conv_w, ffn_w_down):
    weights = dict(w_ada=w_ada, b_ada=b_ada, norm_mix=norm_mix, norm_ffn=norm_ffn, w_in=w_in,
                   gdn_conv_w=gdn_conv_w, gdn_a_log=gdn_a_log, gdn_dt_bias=gdn_dt_bias, gdn_norm=gdn_norm,
                   diff_q_norm=diff_q_norm, diff_k_norm=diff_k_norm, diff_lq1=diff_lq1, diff_lk1=diff_lk1,
                   diff_lq2=diff_lq2, diff_lk2=diff_lk2, diff_subln=diff_subln,
                   s5_a_re=s5_a_re, s5_a_im=s5_a_im, s5_log_dt=s5_log_dt, s5_b_re=s5_b_re, s5_b_im=s5_b_im,
                   s5_c_re=s5_c_re, s5_c_im=s5_c_im, s5_d=s5_d, s5_w_glu=s5_w_glu,
                   mla_q_a_norm=mla_q_a_norm, mla_w_qb=mla_w_qb, mla_q_norm=mla_q_norm,
                   mla_kv_norm=mla_kv_norm, mla_k_norm=mla_k_norm, mla_w_kvb=mla_w_kvb,
                   w_branch=w_branch, w_out=w_out, ffn_w_up=ffn_w_up, ffn_conv_w=ffn_conv_w, ffn_w_down=ffn_w_down)
    n_prompt, n_sample = x_prompt.shape[0], x_sample.shape[0]
    past_len = page_table.shape[1] * PAGE_SIZE
    pos_p = jnp.arange(x_prompt.shape[1], dtype=jnp.int32)
    pos_s = past_len + jnp.arange(x_sample.shape[1], dtype=jnp.int32)

    n_pool = cache_diff_k.shape[1]
    paged = dict(
        diff_k_t=cache_diff_k.transpose(0, 1, 3, 4, 2).reshape(DEPTH, n_pool, 2 * DIFF_HEAD_DIM, PAGE_SIZE),
        diff_v=cache_diff_v,
        mla_ckv=cache_mla_ckv,
        mla_krope_t=cache_mla_krope.transpose(0, 1, 3, 2),
        page_table=page_table)

    c_all = jnp.concatenate([c_prompt, c_sample], axis=0)
    n_mod = -(-c_all.shape[0] // 16) * 16
    c_act = _pad_rows(jax.nn.silu(c_all), n_mod, 0)

    xp, xs = x_prompt, x_sample
    rows_p = {name: [] for name in STATE_NAMES}
    rows_s = {name: [] for name in STATE_NAMES}
    for i in range(DEPTH):
        p = {name: w[i] for name, w in weights.items()}
        p['w_in_bf'] = _reorder_w_in(p['w_in']).astype(BF16)
        for name in ('s5_w_glu', 'mla_w_qb', 'w_branch', 'w_out', 'ffn_w_up', 'ffn_w_down'):
            p[name + '_bf'] = p[name].astype(BF16)
        mod = _mm(c_act, p['w_ada'].astype(BF16))[:n_prompt + n_sample] + p['b_ada']
        zeros_p = dict(gdn=jnp.zeros((n_prompt, GDN_HEADS, GDN_HEAD_DIM, GDN_HEAD_DIM), F32),
                       gdn_conv=jnp.zeros((n_prompt, GDN_CONV - 1, QKV_WIDTH), F32),
                       s5_re=jnp.zeros((n_prompt, S5_GROUPS, S5_STATE), F32),
                       s5_im=jnp.zeros((n_prompt, S5_GROUPS, S5_STATE), F32),
                       ffn_conv=jnp.zeros((n_prompt, FFN_CONV - 1, 2 * D_FF), F32))
        xp, new_p = _layer(i, xp, mod[:n_prompt], pos_p, zeros_p, p, None)
        st_s = dict(gdn=state_gdn[i], gdn_conv=state_gdn_conv[i], s5_re=state_s5_re[i],
                    s5_im=state_s5_im[i], ffn_conv=state_ffn_conv[i])
        xs, new_s = _layer(i, xs, mod[n_prompt:], pos_s, st_s, p, paged)
        for name in STATE_NAMES:
            rows_p[name].append(new_p[name])
            rows_s[name].append(new_s[name])
    out_p = {name: jnp.stack(v, axis=0) for name, v in rows_p.items()}
    out_s = {name: jnp.stack(v, axis=0) for name, v in rows_s.items()}
    return (xp, xs,
            out_p['diff_k'], out_p['diff_v'], out_p['mla_ckv'], out_p['mla_krope'], out_p['gdn'],
            out_p['gdn_conv'], out_p['s5_re'], out_p['s5_im'], out_p['ffn_conv'],
            out_s['diff_k'], out_s['diff_v'], out_s['mla_ckv'], out_s['mla_krope'], out_s['gdn'],
            out_s['gdn_conv'], out_s['s5_re'], out_s['s5_im'], out_s['ffn_conv'])
```

```python
import functools
import math

import jax
import jax.numpy as jnp
from jax import lax
from jax.experimental import pallas as pl
from jax.experimental.pallas import tpu as pltpu

D_MODEL = 2048
DEPTH = 2
PAGE_SIZE = 128
N_BRANCH = 4
BRANCH_WIDTH = D_MODEL // N_BRANCH
GDN_HEADS = 4
GDN_HEAD_DIM = BRANCH_WIDTH // GDN_HEADS
GDN_CONV = 4
GDN_CHUNK = 64
DIFF_HEADS = 4
DIFF_HEAD_DIM = BRANCH_WIDTH // (2 * DIFF_HEADS)
S5_GROUP_CH = 16
S5_GROUPS = BRANCH_WIDTH // S5_GROUP_CH
S5_STATE = 64
MLA_HEADS = 4
MLA_NOPE = 64
MLA_ROPE = 32
MLA_V = BRANCH_WIDTH // MLA_HEADS
MLA_Q_LORA = 384
MLA_KV_LORA = 256
ROPE_THETA = 10000.0
D_FF = 5632
FFN_CONV = 3
NEG_INF = -1e30
EPS = 1e-6

QKV_WIDTH = 3 * GDN_HEADS * GDN_HEAD_DIM
OFF_QKV = 0
OFF_Z = OFF_QKV + QKV_WIDTH
OFF_DQ = OFF_Z + BRANCH_WIDTH
OFF_DK = OFF_DQ + BRANCH_WIDTH
OFF_DV = OFF_DK + 2 * DIFF_HEAD_DIM
OFF_SU = OFF_DV + 2 * DIFF_HEAD_DIM
OFF_MQA = OFF_SU + BRANCH_WIDTH
OFF_MKVA = OFF_MQA + MLA_Q_LORA
OFF_MKR = OFF_MKVA + MLA_KV_LORA
OFF_GB = OFF_MKR + MLA_ROPE
OFF_GA = OFF_GB + GDN_HEADS
OFF_GATES = 4096
IN_WIDTH = OFF_GATES + N_BRANCH * D_MODEL

LANES = 128
VMEM_LIMIT = 56 * 1024 * 1024
PAGES_PER_STEP = 32

F32 = jnp.float32
BF16 = jnp.bfloat16


def _mm_kernel(a_ref, b_ref, o_ref):
    o_ref[...] = jnp.dot(a_ref[...], b_ref[...], preferred_element_type=F32).astype(o_ref.dtype)


def _mm(a, b, *, tm=1024, tn=512, out_dtype=F32):
    m, k = a.shape
    n = b.shape[1]
    tm = min(tm, m)
    tn = min(tn, n)
    assert m % tm == 0 and n % tn == 0, (m, n, tm, tn)
    return pl.pallas_call(
        _mm_kernel,
        grid=(m // tm, n // tn),
        in_specs=[pl.BlockSpec((tm, k), lambda i, j: (i, 0)),
                  pl.BlockSpec((k, tn), lambda i, j: (0, j))],
        out_specs=pl.BlockSpec((tm, tn), lambda i, j: (i, j)),
        out_shape=jax.ShapeDtypeStruct((m, n), out_dtype),
        compiler_params=pltpu.CompilerParams(
            dimension_semantics=("parallel", "parallel"), vmem_limit_bytes=VMEM_LIMIT),
    )(a.astype(BF16), b.astype(BF16))


def _softmax_update(s, v, m_sc, l_sc, acc_sc):
    m_old = m_sc[...]
    m_new = jnp.maximum(m_old, jnp.max(s, axis=-1, keepdims=True))
    p = jnp.exp(s - m_new)
    corr = jnp.exp(m_old - m_new)
    l_sc[...] = l_sc[...] * corr + jnp.sum(p, axis=-1, keepdims=True)
    acc_sc[...] = acc_sc[...] * corr + jnp.dot(p.astype(BF16), v, preferred_element_type=F32)
    m_sc[...] = m_new


def _prompt_attn_kernel(q_ref, k_ref, v_ref, o_ref, m_sc, l_sc, acc_sc, *, n_heads, tq, tk, slopes):
    qi = pl.program_id(2)
    rows = n_heads * tq
    q = q_ref[0, 0].reshape(rows, q_ref.shape[-1])
    m_sc[...] = jnp.full(m_sc.shape, NEG_INF, F32)
    l_sc[...] = jnp.zeros(l_sc.shape, F32)
    acc_sc[...] = jnp.zeros(acc_sc.shape, F32)
    row = lax.broadcasted_iota(jnp.int32, (rows, 1), 0)
    q_pos = qi * tq + row % tq
    if slopes is not None:
        slope = jnp.concatenate([jnp.full((tq, 1), sl, F32) for sl in slopes], axis=0)

    def body(kb, carry):
        start = pl.multiple_of(kb * tk, tk)
        k = k_ref[0, 0, pl.ds(start, tk), :]
        v = v_ref[0, pl.ds(start, tk), :]
        s = lax.dot_general(q, k, (((1,), (1,)), ((), ())), preferred_element_type=F32)
        k_pos = start + lax.broadcasted_iota(jnp.int32, (1, tk), 1)
        rel = q_pos - k_pos
        if slopes is not None:
            s = s - slope * rel.astype(F32)
        s = jnp.where(rel >= 0, s, NEG_INF)
        _softmax_update(s, v, m_sc, l_sc, acc_sc)
        return carry

    n_blocks = (qi * tq + tq + tk - 1) // tk
    lax.fori_loop(0, n_blocks, body, 0)
    out = acc_sc[...] / l_sc[...]
    o_ref[0, 0] = out.reshape(n_heads, tq, out.shape[-1])


def _prompt_attention(q, k, v, slopes, *, tq=256, tk=256):
    b, n_maps, n_heads, length, dk = q.shape
    dv = v.shape[-1]
    rows = n_heads * tq
    kern = functools.partial(_prompt_attn_kernel, n_heads=n_heads, tq=tq, tk=tk, slopes=slopes)
    return pl.pallas_call(
        kern,
        grid=(b, n_maps, length // tq),
        in_specs=[pl.BlockSpec((1, 1, n_heads, tq, dk), lambda bi, mi, qi: (bi, mi, 0, qi, 0)),
                  pl.BlockSpec((1, 1, length, dk), lambda bi, mi, qi: (bi, mi, 0, 0)),
                  pl.BlockSpec((1, length, dv), lambda bi, mi, qi: (bi, 0, 0))],
        out_specs=pl.BlockSpec((1, 1, n_heads, tq, dv), lambda bi, mi, qi: (bi, mi, 0, qi, 0)),
        out_shape=jax.ShapeDtypeStruct((b, n_maps, n_heads, length, dv), F32),
        scratch_shapes=[pltpu.VMEM((rows, 1), F32), pltpu.VMEM((rows, 1), F32), pltpu.VMEM((rows, dv), F32)],
        compiler_params=pltpu.CompilerParams(
            dimension_semantics=("parallel", "parallel", "parallel"), vmem_limit_bytes=VMEM_LIMIT),
    )(q.astype(BF16), k.astype(BF16), v.astype(BF16))


def _paged_copies(pt_ref, bi, ci, slot, layer, pc, specs, sems):
    copies = []
    for j in range(pc):
        page = pt_ref[bi, ci * pc + j]
        for a, (hbm, buf, keys_on_lanes) in enumerate(specs):
            src = hbm.at[layer, page]
            if keys_on_lanes:
                dst = buf.at[slot, pl.ds(0, src.shape[0]), pl.ds(j * PAGE_SIZE, PAGE_SIZE)]
            else:
                dst = buf.at[slot, pl.ds(j * PAGE_SIZE, PAGE_SIZE), :]
            copies.append(pltpu.make_async_copy(src, dst, sems.at[slot, a]))
    return copies


def _paged_pipeline(pt_ref, layer, pc, specs, sems):
    bi, ci = pl.program_id(0), pl.program_id(1)
    nb, nc = pl.num_programs(0), pl.num_programs(1)
    step = bi * nc + ci
    slot = step % 2

    @pl.when(step == 0)
    def _():
        for cp in _paged_copies(pt_ref, bi, ci, slot, layer, pc, specs, sems):
            cp.start()

    @pl.when(step + 1 < nb * nc)
    def _():
        nxt = step + 1
        for cp in _paged_copies(pt_ref, nxt // nc, nxt % nc, 1 - slot, layer, pc, specs, sems):
            cp.start()

    for cp in _paged_copies(pt_ref, bi, ci, slot, layer, pc, specs, sems):
        cp.wait()
    return slot


def _paged_diff_kernel(pt_ref, q_ref, slope_ref, trow_ref, knew_ref, vnew_ref, kc_hbm, vc_hbm, o_ref,
                       kbuf, vbuf, sems, m_sc, l_sc, acc_sc, *, layer, pc, past_len):
    ci = pl.program_id(1)
    slot = _paged_pipeline(pt_ref, layer, pc, [(kc_hbm, kbuf, True), (vc_hbm, vbuf, False)], sems)

    @pl.when(ci == 0)
    def _():
        m_sc[...] = jnp.full(m_sc.shape, NEG_INF, F32)
        l_sc[...] = jnp.zeros(l_sc.shape, F32)
        acc_sc[...] = jnp.zeros(acc_sc.shape, F32)

    q = q_ref[0]
    slope = slope_ref[...]
    t_row = trow_ref[...]
    span = pc * PAGE_SIZE
    s = jnp.dot(q, kbuf[slot].astype(BF16), preferred_element_type=F32)
    k_off = (ci * span + lax.broadcasted_iota(jnp.int32, (1, span), 1)).astype(F32)
    rel = (t_row + float(past_len)) - k_off
    s = s - slope * rel
    _softmax_update(s, vbuf[slot].astype(BF16), m_sc, l_sc, acc_sc)

    @pl.when(ci == pl.num_programs(1) - 1)
    def _():
        s_new = jnp.dot(q, knew_ref[0], preferred_element_type=F32)
        rel_new = t_row - lax.broadcasted_iota(jnp.int32, (1, PAGE_SIZE), 1).astype(F32)
        s_new = jnp.where(rel_new >= 0, s_new - slope * rel_new, NEG_INF)
        _softmax_update(s_new, vnew_ref[0], m_sc, l_sc, acc_sc)
        o_ref[0] = acc_sc[...] / l_sc[...]


def _paged_diff_attention(layer, q, slope, t_row, k_new_t, v_new, cache_k_t, cache_v, page_table, past_len):
    b, rows, feat = q.shape
    dv = cache_v.shape[-1]
    pc = PAGES_PER_STEP
    n_chunks = page_table.shape[1] // pc
    span = pc * PAGE_SIZE
    kern = functools.partial(_paged_diff_kernel, layer=layer, pc=pc, past_len=past_len)
    grid_spec = pltpu.PrefetchScalarGridSpec(
        num_scalar_prefetch=1,
        grid=(b, n_chunks),
        in_specs=[pl.BlockSpec((1, rows, feat), lambda bi, ci, pt: (bi, 0, 0)),
                  pl.BlockSpec((rows, 1), lambda bi, ci, pt: (0, 0)),
                  pl.BlockSpec((rows, 1), lambda bi, ci, pt: (0, 0)),
                  pl.BlockSpec((1, feat, PAGE_SIZE), lambda bi, ci, pt: (bi, 0, 0)),
                  pl.BlockSpec((1, PAGE_SIZE, dv), lambda bi, ci, pt: (bi, 0, 0)),
                  pl.BlockSpec(memory_space=pl.ANY),
                  pl.BlockSpec(memory_space=pl.ANY)],
        out_specs=pl.BlockSpec((1, rows, dv), lambda bi, ci, pt: (bi, 0, 0)),
        scratch_shapes=[pltpu.VMEM((2, feat, span), F32), pltpu.VMEM((2, span, dv), F32),
                        pltpu.SemaphoreType.DMA((2, 2)),
                        pltpu.VMEM((rows, 1), F32), pltpu.VMEM((rows, 1), F32), pltpu.VMEM((rows, dv), F32)])
    return pl.pallas_call(
        kern, grid_spec=grid_spec,
        out_shape=jax.ShapeDtypeStruct((b, rows, dv), F32),
        compiler_params=pltpu.CompilerParams(
            dimension_semantics=("arbitrary", "arbitrary"), vmem_limit_bytes=VMEM_LIMIT),
    )(page_table, q, slope, t_row, k_new_t, v_new, cache_k_t, cache_v)


def _paged_mla_kernel(pt_ref, ql_ref, qr_ref, trow_ref, cnew_ref, rnew_ref, cc_hbm, rc_hbm, o_ref,
                      cbuf, rbuf, sems, m_sc, l_sc, acc_sc, *, layer, pc):
    bi, ci = pl.program_id(0), pl.program_id(1)

    @pl.when((bi == 0) & (ci == 0))
    def _():
        rbuf[...] = jnp.zeros(rbuf.shape, F32)

    slot = _paged_pipeline(pt_ref, layer, pc, [(cc_hbm, cbuf, False), (rc_hbm, rbuf, True)], sems)

    @pl.when(ci == 0)
    def _():
        m_sc[...] = jnp.full(m_sc.shape, NEG_INF, F32)
        l_sc[...] = jnp.zeros(l_sc.shape, F32)
        acc_sc[...] = jnp.zeros(acc_sc.shape, F32)

    ql = ql_ref[0]
    qr = qr_ref[0]
    ckv = cbuf[slot].astype(BF16)
    s = lax.dot_general(ql, ckv, (((1,), (1,)), ((), ())), preferred_element_type=F32)
    s = s + jnp.dot(qr, rbuf[slot].astype(BF16), preferred_element_type=F32)
    _softmax_update(s, ckv, m_sc, l_sc, acc_sc)

    @pl.when(ci == pl.num_programs(1) - 1)
    def _():
        c_new = cnew_ref[0]
        s_new = lax.dot_general(ql, c_new, (((1,), (1,)), ((), ())), preferred_element_type=F32)
        s_new = s_new + jnp.dot(qr, rnew_ref[0], preferred_element_type=F32)
        rel_new = trow_ref[...] - lax.broadcasted_iota(jnp.int32, (1, PAGE_SIZE), 1).astype(F32)
        s_new = jnp.where(rel_new >= 0, s_new, NEG_INF)
        _softmax_update(s_new, c_new, m_sc, l_sc, acc_sc)
        o_ref[0] = acc_sc[...] / l_sc[...]


def _paged_mla_attention(layer, q_lat, q_rope, t_row, c_new, r_new_t, cache_ckv, cache_krope_t, page_table):
    b, rows, lat = q_lat.shape
    pc = PAGES_PER_STEP
    n_chunks = page_table.shape[1] // pc
    span = pc * PAGE_SIZE
    kern = functools.partial(_paged_mla_kernel, layer=layer, pc=pc)
    grid_spec = pltpu.PrefetchScalarGridSpec(
        num_scalar_prefetch=1,
        grid=(b, n_chunks),
        in_specs=[pl.BlockSpec((1, rows, lat), lambda bi, ci, pt: (bi, 0, 0)),
                  pl.BlockSpec((1, rows, LANES), lambda bi, ci, pt: (bi, 0, 0)),
                  pl.BlockSpec((rows, 1), lambda bi, ci, pt: (0, 0)),
                  pl.BlockSpec((1, PAGE_SIZE, lat), lambda bi, ci, pt: (bi, 0, 0)),
                  pl.BlockSpec((1, LANES, PAGE_SIZE), lambda bi, ci, pt: (bi, 0, 0)),
                  pl.BlockSpec(memory_space=pl.ANY),
                  pl.BlockSpec(memory_space=pl.ANY)],
        out_specs=pl.BlockSpec((1, rows, lat), lambda bi, ci, pt: (bi, 0, 0)),
        scratch_shapes=[pltpu.VMEM((2, span, lat), F32), pltpu.VMEM((2, LANES, span), F32),
                        pltpu.SemaphoreType.DMA((2, 2)),
                        pltpu.VMEM((rows, 1), F32), pltpu.VMEM((rows, 1), F32), pltpu.VMEM((rows, lat), F32)])
    return pl.pallas_call(
        kern, grid_spec=grid_spec,
        out_shape=jax.ShapeDtypeStruct((b, rows, lat), F32),
        compiler_params=pltpu.CompilerParams(
            dimension_semantics=("arbitrary", "arbitrary"), vmem_limit_bytes=VMEM_LIMIT),
    )(page_table, q_lat, q_rope, t_row, c_new, r_new_t, cache_ckv, cache_krope_t)


def _rmsnorm(x, g):
    xf = x.astype(F32)
    y = xf * lax.rsqrt(jnp.mean(xf * xf, axis=-1, keepdims=True) + EPS)
    return (y * g.astype(F32)).astype(x.dtype)


def _l2norm(x):
    xf = x.astype(F32)
    return xf * lax.rsqrt(jnp.sum(xf * xf, axis=-1, keepdims=True) + EPS)


def _rope(x, pos):
    half = x.shape[-1] // 2
    inv = ROPE_THETA ** (-jnp.arange(half, dtype=F32) / half)
    ang = pos.astype(F32)[:, None] * inv
    ang = ang.reshape((pos.shape[0],) + (1,) * (x.ndim - 3) + (half,))
    cos, sin = jnp.cos(ang), jnp.sin(ang)
    xf = x.astype(F32)
    x1, x2 = xf[..., :half], xf[..., half:]
    return jnp.concatenate([x1 * cos - x2 * sin, x2 * cos + x1 * sin], axis=-1).astype(x.dtype)


def _alibi_slopes(n_heads):
    return [2.0 ** (-8.0 * (h + 1) / n_heads) for h in range(n_heads)]


def _causal_dwconv(x, buf, w):
    width, length = w.shape[0], x.shape[1]
    xx = jnp.concatenate([buf.astype(x.dtype), x], axis=1)
    y = w[0] * xx[:, :length]
    for j in range(1, width):
        y = y + w[j] * xx[:, j:j + length]
    return y, xx[:, length:]


def _gated_delta_rule(q, k, v, g, beta, s0):
    b, length, h, _ = q.shape
    c = math.gcd(length, GDN_CHUNK)
    n = length // c

    def chunks(t):
        t = t.reshape((b, n, c, h) + t.shape[3:])
        return jnp.moveaxis(t, (1, 3), (0, 2))

    qc, kc, vc, bc = chunks(q), chunks(k), chunks(v), chunks(beta)
    gc = jnp.cumsum(chunks(g), axis=-1)
    tril = jnp.tril(jnp.ones((c, c), dtype=bool))
    strict = jnp.tril(jnp.ones((c, c), dtype=bool), k=-1)
    diff = gc[..., :, None] - gc[..., None, :]
    decay = jnp.where(tril, jnp.exp(jnp.where(tril, diff, 0.0)), 0.0)
    k_beta = kc * bc[..., None]
    a = jnp.where(strict, jnp.einsum('nbhid,nbhjd->nbhij', k_beta, kc) * decay, 0.0)
    lmat = a + jnp.eye(c, dtype=F32)
    u = lax.linalg.triangular_solve(lmat, vc * bc[..., None], left_side=True, lower=True, unit_diagonal=True)
    w = lax.linalg.triangular_solve(lmat, k_beta * jnp.exp(gc)[..., None], left_side=True, lower=True,
                                    unit_diagonal=True)
    qk = jnp.where(tril, jnp.einsum('nbhid,nbhjd->nbhij', qc, kc) * decay, 0.0)

    def step(s, xs):
        q_i, k_i, u_i, w_i, g_i, qk_i = xs
        v_new = u_i - jnp.einsum('bhcd,bhdv->bhcv', w_i, s)
        o = (jnp.einsum('bhcd,bhdv->bhcv', q_i * jnp.exp(g_i)[..., None], s)
             + jnp.einsum('bhij,bhjv->bhiv', qk_i, v_new))
        g_last = g_i[..., -1]
        k_dec = k_i * jnp.exp(g_last[..., None] - g_i)[..., None]
        s = s * jnp.exp(g_last)[..., None, None] + jnp.einsum('bhcd,bhcv->bhdv', k_dec, v_new)
        return s, o

    s_final, o = lax.scan(step, s0, (qc, kc, u, w, gc, qk))
    o = jnp.moveaxis(o, (0, 2), (1, 3)).reshape(b, length, h, -1)
    return o, s_final


def _merge_kernel(ya_ref, yb_ref, yc_ref, yd_ref, w_ref, ga_ref, gb_ref, gc_ref, gd_ref, o_ref):
    acc = None
    for n, (y_ref, g_ref) in enumerate(((ya_ref, ga_ref), (yb_ref, gb_ref), (yc_ref, gc_ref), (yd_ref, gd_ref))):
        z = jnp.dot(y_ref[...], w_ref[n], preferred_element_type=F32)
        term = z / (1.0 + jnp.exp(-g_ref[...]))
        acc = term if acc is None else acc + term
    o_ref[...] = acc.astype(o_ref.dtype)


def _merge_branches(ys, proj, w_branch, *, tm=512, tn=512):
    tokens = proj.shape[0]
    tm = min(tm, tokens)
    gate_block = OFF_GATES // tn
    per_branch = D_MODEL // tn
    y_spec = pl.BlockSpec((tm, BRANCH_WIDTH), lambda i, j: (i, 0))
    gate_specs = [pl.BlockSpec((tm, tn), functools.partial(lambda i, j, n: (i, gate_block + n * per_branch + j), n=n))
                  for n in range(N_BRANCH)]
    return pl.pallas_call(
        _merge_kernel,
        grid=(tokens // tm, D_MODEL // tn),
        in_specs=[y_spec] * N_BRANCH + [pl.BlockSpec((N_BRANCH, BRANCH_WIDTH, tn), lambda i, j: (0, 0, j))]
        + gate_specs,
        out_specs=pl.BlockSpec((tm, tn), lambda i, j: (i, j)),
        out_shape=jax.ShapeDtypeStruct((tokens, D_MODEL), BF16),
        compiler_params=pltpu.CompilerParams(
            dimension_semantics=("parallel", "parallel"), vmem_limit_bytes=VMEM_LIMIT),
    )(*[y.astype(BF16) for y in ys], w_branch, proj, proj, proj, proj)


S5_WIDTH = S5_GROUPS * S5_STATE
S5_LANE_CHUNK = 1024


def _s5_scan_kernel(bu_ref, a_ref, x0_ref, o_ref, last_ref, x_sc, *, rows):
    ti = pl.program_id(1)

    @pl.when(ti == 0)
    def _():
        x_sc[...] = x0_ref[0]

    for c0 in range(0, S5_WIDTH, S5_LANE_CHUNK):
        re = pl.ds(c0, S5_LANE_CHUNK)
        im = pl.ds(S5_WIDTH + c0, S5_LANE_CHUNK)
        a_re = a_ref[:, re]
        a_im = a_ref[:, im]

        def body(r, carry, re=re, im=im, a_re=a_re, a_im=a_im):
            x_re, x_im = carry
            row = pl.ds(r, 1)
            n_re = a_re * x_re - a_im * x_im + bu_ref[row, re]
            n_im = a_re * x_im + a_im * x_re + bu_ref[row, im]
            o_ref[row, re] = n_re
            o_ref[row, im] = n_im
            return n_re, n_im

        x_re, x_im = lax.fori_loop(0, rows, body, (x_sc[:, re], x_sc[:, im]), unroll=8)
        x_sc[:, re] = x_re
        x_sc[:, im] = x_im

    @pl.when(ti == pl.num_programs(1) - 1)
    def _():
        last_ref[0] = x_sc[...]


def _s5_scan(bu, a, x0, n_seq, length):
    rows = min(length, 256)
    width = bu.shape[1]
    tiles = length // rows
    return pl.pallas_call(
        functools.partial(_s5_scan_kernel, rows=rows),
        grid=(n_seq, tiles),
        in_specs=[pl.BlockSpec((rows, width), lambda s, t: (s * tiles + t, 0)),
                  pl.BlockSpec((1, width), lambda s, t: (0, 0)),
                  pl.BlockSpec((1, 1, width), lambda s, t: (s, 0, 0))],
        out_specs=[pl.BlockSpec((rows, width), lambda s, t: (s * tiles + t, 0)),
                   pl.BlockSpec((1, 1, width), lambda s, t: (s, 0, 0))],
        out_shape=[jax.ShapeDtypeStruct(bu.shape, F32), jax.ShapeDtypeStruct((n_seq, 1, width), F32)],
        scratch_shapes=[pltpu.VMEM((1, width), F32)],
        compiler_params=pltpu.CompilerParams(
            dimension_semantics=("parallel", "arbitrary"), vmem_limit_bytes=VMEM_LIMIT),
    )(bu, a, x0)


def _s5(u, x0_re, x0_im, p):
    b, length, _ = u.shape
    tokens = b * length
    lam = lax.complex(p['s5_a_re'].astype(F32), p['s5_a_im'].astype(F32))
    dt = jnp.exp(p['s5_log_dt'].astype(F32))[:, None]
    lam_bar = jnp.exp(lam * dt)
    b_bar = ((lam_bar - 1.0) / lam)[..., None] * lax.complex(p['s5_b_re'].astype(F32), p['s5_b_im'].astype(F32))
    eye = jnp.eye(S5_GROUPS, dtype=F32)

    def in_mat(m):
        return (m[:, None] * eye[:, :, None, None]).transpose(0, 3, 1, 2).reshape(BRANCH_WIDTH, S5_WIDTH)

    def out_mat(m):
        return (m[:, None] * eye[:, :, None, None]).transpose(0, 3, 1, 2).reshape(S5_WIDTH, BRANCH_WIDTH)

    b_mat = jnp.concatenate([in_mat(jnp.real(b_bar)), in_mat(jnp.imag(b_bar))], axis=1)
    c_mat = jnp.concatenate([out_mat(p['s5_c_re'].astype(F32)), -out_mat(p['s5_c_im'].astype(F32))], axis=0)
    a = jnp.concatenate([jnp.real(lam_bar).reshape(1, S5_WIDTH), jnp.imag(lam_bar).reshape(1, S5_WIDTH)], axis=1)
    x0 = jnp.concatenate([x0_re.astype(F32).reshape(b, 1, S5_WIDTH), x0_im.astype(F32).reshape(b, 1, S5_WIDTH)],
                         axis=2)
    u2 = u.reshape(tokens, BRANCH_WIDTH)
    bu = _mm(u2, b_mat)
    states, last = _s5_scan(bu, a, x0, b, length)
    y = _mm(states, c_mat) + p['s5_d'].astype(F32) * u2
    v = jax.nn.gelu(y)
    glu = _mm(v, p['s5_w_glu_bf']).reshape(b, length, 2 * BRANCH_WIDTH)
    glu_a, glu_b = jnp.split(glu, 2, axis=-1)
    last_re = last[:, 0, :S5_WIDTH].reshape(b, S5_GROUPS, S5_STATE)
    last_im = last[:, 0, S5_WIDTH:].reshape(b, S5_GROUPS, S5_STATE)
    return glu_a * jax.nn.sigmoid(glu_b), last_re.astype(x0_re.dtype), last_im.astype(x0_im.dtype)


def _attend_prompt_diff(dq, dk, d_v):
    q = (dq * DIFF_HEAD_DIM ** -0.5).transpose(0, 2, 3, 1, 4)
    k = dk.transpose(0, 2, 1, 3)
    out = _prompt_attention(q, k, d_v, _alibi_slopes(DIFF_HEADS))
    return out.transpose(0, 3, 1, 2, 4)


def _attend_prompt_mla(q_cat, k_cat, ckv):
    pad = 3 * LANES - q_cat.shape[-1]
    q = (q_cat * (MLA_NOPE + MLA_ROPE) ** -0.5).transpose(0, 2, 1, 3)[:, None]
    q = jnp.pad(q, ((0, 0),) * 4 + ((0, pad),))
    k = jnp.pad(k_cat, ((0, 0), (0, 0), (0, pad)))[:, None]
    out = _prompt_attention(q, k, ckv, None)
    return out[:, 0].transpose(0, 2, 1, 3)


def _pad_rows(x, n, axis):
    pad = [(0, 0)] * x.ndim
    pad[axis] = (0, n - x.shape[axis])
    return jnp.pad(x, pad)


def _attend_paged_diff(layer, dq, dk, d_v, cache_k_t, cache_v, page_table):
    b, t = dq.shape[:2]
    d = DIFF_HEAD_DIM
    past_len = page_table.shape[1] * PAGE_SIZE
    qs = (dq * d ** -0.5).transpose(0, 2, 1, 3, 4).reshape(b, 2, t * DIFF_HEADS, d)
    zeros = jnp.zeros_like(qs[:, 0])
    q = jnp.concatenate([jnp.concatenate([qs[:, 0], zeros], axis=-1),
                         jnp.concatenate([zeros, qs[:, 1]], axis=-1)], axis=1).astype(BF16)
    slope = jnp.tile(jnp.asarray(_alibi_slopes(DIFF_HEADS), F32), 2 * t)[:, None]
    t_row = jnp.tile(jnp.repeat(jnp.arange(t, dtype=F32), DIFF_HEADS), 2)[:, None]
    k_new_t = _pad_rows(dk.reshape(b, t, 2 * d), PAGE_SIZE, 1).transpose(0, 2, 1).astype(BF16)
    v_new = _pad_rows(d_v, PAGE_SIZE, 1).astype(BF16)
    out = _paged_diff_attention(layer, q, slope, t_row, k_new_t, v_new, cache_k_t, cache_v, page_table, past_len)
    return out.reshape(b, 2, t, DIFF_HEADS, 2 * d).transpose(0, 2, 1, 3, 4)


def _attend_paged_mla(layer, q_lat, q_rope, ckv, krope, cache_ckv, cache_krope_t, page_table):
    b, t = q_lat.shape[:2]
    scale = (MLA_NOPE + MLA_ROPE) ** -0.5
    ql = (q_lat * scale).reshape(b, t * MLA_HEADS, MLA_KV_LORA).astype(BF16)
    qr = _pad_rows((q_rope * scale).reshape(b, t * MLA_HEADS, MLA_ROPE), LANES, 2).astype(BF16)
    t_row = jnp.repeat(jnp.arange(t, dtype=F32), MLA_HEADS)[:, None]
    c_new = _pad_rows(ckv, PAGE_SIZE, 1).astype(BF16)
    r_new_t = _pad_rows(_pad_rows(krope, PAGE_SIZE, 1), LANES, 2).transpose(0, 2, 1).astype(BF16)
    out = _paged_mla_attention(layer, ql, qr, t_row, c_new, r_new_t, cache_ckv, cache_krope_t, page_table)
    return out.reshape(b, t, MLA_HEADS, MLA_KV_LORA)


def _layer(i, x, mod, pos, st, p, paged):
    b, length, _ = x.shape
    tokens = b * length
    sh_m, sc_m, gt_m, sh_f, sc_f, gt_f = jnp.split(mod[:, None, :], 6, axis=-1)
    h = _rmsnorm(x, p['norm_mix']) * (1 + sc_m) + sh_m
    proj = _mm(h.reshape(tokens, D_MODEL), p['w_in_bf']).reshape(b, length, IN_WIDTH)
    g_qkv = proj[..., OFF_QKV:OFF_Z]
    g_z = proj[..., OFF_Z:OFF_DQ]
    d_q = proj[..., OFF_DQ:OFF_DK]
    d_k = proj[..., OFF_DK:OFF_DV]
    d_v = proj[..., OFF_DV:OFF_SU]
    s_u = proj[..., OFF_SU:OFF_MQA]
    m_qa = proj[..., OFF_MQA:OFF_MKVA]
    m_kva = proj[..., OFF_MKVA:OFF_MKR]
    m_kr = proj[..., OFF_MKR:OFF_GB]
    g_b = proj[..., OFF_GB:OFF_GA]
    g_a = proj[..., OFF_GA:OFF_GA + GDN_HEADS]

    qkv, gdn_conv_new = _causal_dwconv(g_qkv, st['gdn_conv'], p['gdn_conv_w'])
    qkv = jax.nn.silu(qkv).reshape(b, length, 3, GDN_HEADS, GDN_HEAD_DIM)
    gq = _l2norm(qkv[:, :, 0]) * GDN_HEAD_DIM ** -0.5
    gk = _l2norm(qkv[:, :, 1])
    gv = qkv[:, :, 2].astype(F32)
    beta = jax.nn.sigmoid(g_b.astype(F32))
    log_decay = -jnp.exp(p['gdn_a_log'].astype(F32)) * jax.nn.softplus(g_a.astype(F32) + p['gdn_dt_bias'].astype(F32))
    o_a, gdn_state_new = _gated_delta_rule(gq, gk, gv, log_decay, beta, st['gdn'].astype(F32))
    o_a = _rmsnorm(o_a, p['gdn_norm']) * jax.nn.silu(g_z.reshape(b, length, GDN_HEADS, GDN_HEAD_DIM).astype(F32))
    y_a = o_a.reshape(b, length, BRANCH_WIDTH)

    dq = _rmsnorm(d_q.reshape(b, length, DIFF_HEADS, 2, DIFF_HEAD_DIM).swapaxes(2, 3), p['diff_q_norm'])
    dk = _rmsnorm(d_k.reshape(b, length, 2, DIFF_HEAD_DIM), p['diff_k_norm'])
    lam_init = 0.8 - 0.6 * math.exp(-0.3 * i)
    lam = (jnp.exp(jnp.sum(p['diff_lq1'].astype(F32) * p['diff_lk1'].astype(F32)))
           - jnp.exp(jnp.sum(p['diff_lq2'].astype(F32) * p['diff_lk2'].astype(F32))) + lam_init)
    if paged is None:
        att_b = _attend_prompt_diff(dq, dk, d_v)
    else:
        att_b = _attend_paged_diff(i, dq, dk, d_v, paged['diff_k_t'], paged['diff_v'], paged['page_table'])
    o_b = att_b[:, :, 0] - lam * att_b[:, :, 1]
    o_b = _rmsnorm(o_b, p['diff_subln']) * (1.0 - lam_init)
    y_b = o_b.reshape(b, length, BRANCH_WIDTH)

    y_c, s5_re_new, s5_im_new = _s5(s_u, st['s5_re'], st['s5_im'], p)

    mq = _mm(_rmsnorm(m_qa, p['mla_q_a_norm']).reshape(tokens, MLA_Q_LORA), p['mla_w_qb_bf'])
    mq = _rmsnorm(mq.reshape(b, length, MLA_HEADS, MLA_NOPE + MLA_ROPE), p['mla_q_norm'])
    q_rope = _rope(mq[..., MLA_NOPE:], pos)
    q_lat = jnp.einsum('blhd,rhd->blhr', mq[..., :MLA_NOPE], p['mla_w_kvb'][..., :MLA_NOPE])
    ckv = _rmsnorm(m_kva, p['mla_kv_norm'])
    krope = _rope(_rmsnorm(m_kr, p['mla_k_norm']), pos)
    if paged is None:
        q_cat = jnp.concatenate([q_lat, q_rope], axis=-1)
        k_cat = jnp.concatenate([ckv, krope], axis=-1)
        att_d = _attend_prompt_mla(q_cat, k_cat, ckv)
    else:
        att_d = _attend_paged_mla(i, q_lat, q_rope, ckv, krope, paged['mla_ckv'], paged['mla_krope_t'],
                                  paged['page_table'])
    y_d = jnp.einsum('blhr,rhv->blhv', att_d, p['mla_w_kvb'][..., MLA_NOPE:]).reshape(b, length, BRANCH_WIDTH)

    merged = _merge_branches([y.reshape(tokens, BRANCH_WIDTH) for y in (y_a, y_b, y_c, y_d)],
                             proj.reshape(tokens, IN_WIDTH), p['w_branch_bf'])
    x = x + gt_m * _mm(merged, p['w_out_bf']).reshape(b, length, D_MODEL)

    hf = _rmsnorm(x, p['norm_ffn']) * (1 + sc_f) + sh_f
    up = _mm(hf.reshape(tokens, D_MODEL), p['ffn_w_up_bf']).reshape(b, length, 2 * D_FF)
    up, ffn_conv_new = _causal_dwconv(up, st['ffn_conv'], p['ffn_conv_w'])
    u_gate, u_val = jnp.split(up, 2, axis=-1)
    act = (jax.nn.silu(u_gate) * u_val).reshape(tokens, D_FF)
    x = x + gt_f * _mm(act, p['ffn_w_down_bf'], tm=512).reshape(b, length, D_MODEL)

    new = dict(diff_k=dk, diff_v=d_v, mla_ckv=ckv, mla_krope=krope,
               gdn=gdn_state_new.astype(st['gdn'].dtype), gdn_conv=gdn_conv_new,
               s5_re=s5_re_new, s5_im=s5_im_new, ffn_conv=ffn_conv_new)
    return x, new


STATE_NAMES = ('diff_k', 'diff_v', 'mla_ckv', 'mla_krope', 'gdn', 'gdn_conv', 's5_re', 's5_im', 'ffn_conv')


def _reorder_w_in(w):
    pad = jnp.zeros((w.shape[0], OFF_GATES - (OFF_GA + GDN_HEADS)), w.dtype)
    return jnp.concatenate([w[:, :2048], w[:, 2056:3976], w[:, 3976:4008], w[:, 2048:2056], pad, w[:, 4008:]],
                           axis=1)


def kernel(x_prompt, x_sample, cache_diff_k, cache_diff_v, cache_mla_ckv, cache_mla_krope, state_gdn, state_gdn_conv, state_s5_re, state_s5_im, state_ffn_conv, page_table, c_prompt, c_sample, w_ada, b_ada, norm_mix, norm_ffn, w_in, gdn_conv_w, gdn_a_log, gdn_dt_bias, gdn_norm, diff_q_norm, diff_k_norm, diff_lq1, diff_lk1, diff_lq2, diff_lk2, diff_subln, s5_a_re, s5_a_im, s5_log_dt, s5_b_re, s5_b_im, s5_c_re, s5_c_im, s5_d, s5_w_glu, mla_q_a_norm, mla_w_qb, mla_q_norm, mla_kv_norm, mla_k_norm, mla_w_kvb, w_branch, w_out, ffn_w_up, ffn_conv_w, ffn_w_down):
    weights = dict(w_ada=w_ada, b_ada=b_ada, norm_mix=norm_mix, norm_ffn=norm_ffn, w_in=w_in,
                   gdn_conv_w=gdn_conv_w, gdn_a_log=gdn_a_log, gdn_dt_bias=gdn_dt_bias, gdn_norm=gdn_norm,
                   diff_q_norm=diff_q_norm, diff_k_norm=diff_k_norm, diff_lq1=diff_lq1, diff_lk1=diff_lk1,
                   diff_lq2=diff_lq2, diff_lk2=diff_lk2, diff_subln=diff_subln,
                   s5_a_re=s5_a_re, s5_a_im=s5_a_im, s5_log_dt=s5_log_dt, s5_b_re=s5_b_re, s5_b_im=s5_b_im,
                   s5_c_re=s5_c_re, s5_c_im=s5_c_im, s5_d=s5_d, s5_w_glu=s5_w_glu,
                   mla_q_a_norm=mla_q_a_norm, mla_w_qb=mla_w_qb, mla_q_norm=mla_q_norm,
                   mla_kv_norm=mla_kv_norm, mla_k_norm=mla_k_norm, mla_w_kvb=mla_w_kvb,
                   w_branch=w_branch, w_out=w_out, ffn_w_up=ffn_w_up, ffn_conv_w=ffn_conv_w, ffn_w_down=ffn_w_down)
    n_prompt, n_sample = x_prompt.shape[0], x_sample.shape[0]
    past_len = page_table.shape[1] * PAGE_SIZE
    pos_p = jnp.arange(x_prompt.shape[1], dtype=jnp.int32)
    pos_s = past_len + jnp.arange(x_sample.shape[1], dtype=jnp.int32)

    n_pool = cache_diff_k.shape[1]
    paged = dict(
        diff_k_t=cache_diff_k.transpose(0, 1, 3, 4, 2).reshape(DEPTH, n_pool, 2 * DIFF_HEAD_DIM, PAGE_SIZE),
        diff_v=cache_diff_v,
        mla_ckv=cache_mla_ckv,
        mla_krope_t=cache_mla_krope.transpose(0, 1, 3, 2),
        page_table=page_table)

    c_all = jnp.concatenate([c_prompt, c_sample], axis=0)
    n_mod = -(-c_all.shape[0] // 16) * 16
    c_act = _pad_rows(jax.nn.silu(c_all), n_mod, 0)

    xp, xs = x_prompt, x_sample
    rows_p = {name: [] for name in STATE_NAMES}
    rows_s = {name: [] for name in STATE_NAMES}
    for i in range(DEPTH):
        p = {name: w[i] for name, w in weights.items()}
        p['w_in_bf'] = _reorder_w_in(p['w_in']).astype(BF16)
        for name in ('s5_w_glu', 'mla_w_qb', 'w_branch', 'w_out', 'ffn_w_up', 'ffn_w_down'):
            p[name + '_bf'] = p[name].astype(BF16)
        mod = _mm(c_act, p['w_ada'].astype(BF16))[:n_prompt + n_sample] + p['b_ada']
        zeros_p = dict(gdn=jnp.zeros((n_prompt, GDN_HEADS, GDN_HEAD_DIM, GDN_HEAD_DIM), F32),
                       gdn_conv=jnp.zeros((n_prompt, GDN_CONV - 1, QKV_WIDTH), F32),
                       s5_re=jnp.zeros((n_prompt, S5_GROUPS, S5_STATE), F32),
                       s5_im=jnp.zeros((n_prompt, S5_GROUPS, S5_STATE), F32),
                       ffn_conv=jnp.zeros((n_prompt, FFN_CONV - 1, 2 * D_FF), F32))
        xp, new_p = _layer(i, xp, mod[:n_prompt], pos_p, zeros_p, p, None)
        st_s = dict(gdn=state_gdn[i], gdn_conv=state_gdn_conv[i], s5_re=state_s5_re[i],
                    s5_im=state_s5_im[i], ffn_conv=state_ffn_conv[i])
        xs, new_s = _layer(i, xs, mod[n_prompt:], pos_s, st_s, p, paged)
        for name in STATE_NAMES:
            rows_p[name].append(new_p[name])
            rows_s[name].append(new_s[name])
    out_p = {name: jnp.stack(v, axis=0) for name, v in rows_p.items()}
    out_s = {name: jnp.stack(v, axis=0) for name, v in rows_s.items()}
    return (xp, xs,
            out_p['diff_k'], out_p['diff_v'], out_p['mla_ckv'], out_p['mla_krope'], out_p['gdn'],
            out_p['gdn_conv'], out_p['s5_re'], out_p['s5_im'], out_p['ffn_conv'],
            out_s['diff_k'], out_s['diff_v'], out_s['mla_ckv'], out_s['mla_krope'], out_s['gdn'],
            out_s['gdn_conv'], out_s['s5_re'], out_s['s5_im'], out_s['ffn_conv'])
```

```python
import functools
import math

import jax
import jax.numpy as jnp
from jax import lax
from jax.experimental import pallas as pl
from jax.experimental.pallas import tpu as pltpu

D_MODEL = 2048
DEPTH = 2
PAGE_SIZE = 128
N_BRANCH = 4
BRANCH_WIDTH = D_MODEL // N_BRANCH
GDN_HEADS = 4
GDN_HEAD_DIM = BRANCH_WIDTH // GDN_HEADS
GDN_CONV = 4
GDN_CHUNK = 64
DIFF_HEADS = 4
DIFF_HEAD_DIM = BRANCH_WIDTH // (2 * DIFF_HEADS)
S5_GROUP_CH = 16
S5_GROUPS = BRANCH_WIDTH // S5_GROUP_CH
S5_STATE = 64
MLA_HEADS = 4
MLA_NOPE = 64
MLA_ROPE = 32
MLA_V = BRANCH_WIDTH // MLA_HEADS
MLA_Q_LORA = 384
MLA_KV_LORA = 256
ROPE_THETA = 10000.0
D_FF = 5632
FFN_CONV = 3
NEG_INF = -1e30
EPS = 1e-6

QKV_WIDTH = 3 * GDN_HEADS * GDN_HEAD_DIM
OFF_QKV = 0
OFF_Z = OFF_QKV + QKV_WIDTH
OFF_DQ = OFF_Z + BRANCH_WIDTH
OFF_DK = OFF_DQ + BRANCH_WIDTH
OFF_DV = OFF_DK + 2 * DIFF_HEAD_DIM
OFF_SU = OFF_DV + 2 * DIFF_HEAD_DIM
OFF_MQA = OFF_SU + BRANCH_WIDTH
OFF_MKVA = OFF_MQA + MLA_Q_LORA
OFF_MKR = OFF_MKVA + MLA_KV_LORA
OFF_GB = OFF_MKR + MLA_ROPE
OFF_GA = OFF_GB + GDN_HEADS
OFF_GATES = 4096
IN_WIDTH = OFF_GATES + N_BRANCH * D_MODEL

LANES = 128
VMEM_LIMIT = 56 * 1024 * 1024
PAGES_PER_STEP = 64
KEY_SUB = 512

F32 = jnp.float32
BF16 = jnp.bfloat16


def _mm_kernel(a_ref, b_ref, o_ref):
    o_ref[...] = jnp.dot(a_ref[...], b_ref[...], preferred_element_type=F32).astype(o_ref.dtype)


def _mm(a, b, *, tm=1024, tn=512, out_dtype=F32):
    m, k = a.shape
    n = b.shape[1]
    tm = min(tm, m)
    tn = min(tn, n)
    assert m % tm == 0 and n % tn == 0, (m, n, tm, tn)
    return pl.pallas_call(
        _mm_kernel,
        grid=(m // tm, n // tn),
        in_specs=[pl.BlockSpec((tm, k), lambda i, j: (i, 0)),
                  pl.BlockSpec((k, tn), lambda i, j: (0, j))],
        out_specs=pl.BlockSpec((tm, tn), lambda i, j: (i, j)),
        out_shape=jax.ShapeDtypeStruct((m, n), out_dtype),
        compiler_params=pltpu.CompilerParams(
            dimension_semantics=("parallel", "parallel"), vmem_limit_bytes=VMEM_LIMIT),
    )(a.astype(BF16), b.astype(BF16))


def _softmax_update(s, v, m_sc, l_sc, acc_sc):
    m_old = m_sc[...]
    m_new = jnp.maximum(m_old, jnp.max(s, axis=-1, keepdims=True))
    p = jnp.exp(s - m_new)
    corr = jnp.exp(m_old - m_new)
    l_sc[...] = l_sc[...] * corr + jnp.sum(p, axis=-1, keepdims=True)
    acc_sc[...] = acc_sc[...] * corr + jnp.dot(p.astype(BF16), v, preferred_element_type=F32)
    m_sc[...] = m_new


Q_SUB = 32


def _prompt_attn_kernel(q_ref, k_ref, v_ref, o_ref, m_sc, l_sc, acc_sc, *, n_heads, tq, tk, slopes):
    qi = pl.program_id(2)
    rows = n_heads * Q_SUB
    row = lax.broadcasted_iota(jnp.int32, (rows, 1), 0)
    k_idx = lax.broadcasted_iota(jnp.int32, (1, tk), 1)
    if slopes is not None:
        slope = jnp.concatenate([jnp.full((Q_SUB, 1), sl, F32) for sl in slopes], axis=0)
        bias_in_block = slope * k_idx.astype(F32)

    for qs in range(tq // Q_SUB):
        tok = pl.ds(qs * Q_SUB, Q_SUB)
        q_start = qi * tq + qs * Q_SUB
        q = q_ref[0, 0, :, tok, :].reshape(rows, q_ref.shape[-1])
        q_pos = q_start + row % Q_SUB
        m_sc[...] = jnp.full(m_sc.shape, NEG_INF, F32)
        l_sc[...] = jnp.zeros(l_sc.shape, F32)
        acc_sc[...] = jnp.zeros(acc_sc.shape, F32)

        def step(kb, carry, q=q, q_pos=q_pos, masked=False):
            start = pl.multiple_of(kb * tk, tk)
            k = k_ref[0, 0, pl.ds(start, tk), :]
            v = v_ref[0, pl.ds(start, tk), :]
            s = lax.dot_general(q, k, (((1,), (1,)), ((), ())), preferred_element_type=F32)
            if slopes is not None:
                s = s + bias_in_block + slope * (start - q_pos).astype(F32)
            if masked:
                s = jnp.where(q_pos - (start + k_idx) >= 0, s, NEG_INF)
            _softmax_update(s, v, m_sc, l_sc, acc_sc)
            return carry

        n_full = (q_start + 1) // tk
        n_blocks = (q_start + Q_SUB + tk - 1) // tk
        lax.fori_loop(0, n_full, step, 0)
        lax.fori_loop(n_full, n_blocks, functools.partial(step, masked=True), 0)
        out = acc_sc[...] / l_sc[...]
        o_ref[0, 0, :, tok, :] = out.reshape(n_heads, Q_SUB, out.shape[-1])


def _prompt_attention(q, k, v, slopes, *, tq=256, tk=256):
    b, n_maps, n_heads, length, dk = q.shape
    dv = v.shape[-1]
    rows = n_heads * Q_SUB
    kern = functools.partial(_prompt_attn_kernel, n_heads=n_heads, tq=tq, tk=tk, slopes=slopes)
    return pl.pallas_call(
        kern,
        grid=(b, n_maps, length // tq),
        in_specs=[pl.BlockSpec((1, 1, n_heads, tq, dk), lambda bi, mi, qi: (bi, mi, 0, qi, 0)),
                  pl.BlockSpec((1, 1, length, dk), lambda bi, mi, qi: (bi, mi, 0, 0)),
                  pl.BlockSpec((1, length, dv), lambda bi, mi, qi: (bi, 0, 0))],
        out_specs=pl.BlockSpec((1, 1, n_heads, tq, dv), lambda bi, mi, qi: (bi, mi, 0, qi, 0)),
        out_shape=jax.ShapeDtypeStruct((b, n_maps, n_heads, length, dv), F32),
        scratch_shapes=[pltpu.VMEM((rows, 1), F32), pltpu.VMEM((rows, 1), F32), pltpu.VMEM((rows, dv), F32)],
        compiler_params=pltpu.CompilerParams(
            dimension_semantics=("parallel", "parallel", "parallel"), vmem_limit_bytes=VMEM_LIMIT),
    )(q.astype(BF16), k.astype(BF16), v.astype(BF16))


def _paged_copies(pt_ref, bi, ci, slot, layer, pc, specs, sems):
    copies = []
    for j in range(pc):
        page = pt_ref[bi, ci * pc + j]
        for a, (hbm, buf, keys_on_lanes) in enumerate(specs):
            src = hbm.at[layer, page]
            if keys_on_lanes:
                dst = buf.at[slot, pl.ds(0, src.shape[0]), pl.ds(j * PAGE_SIZE, PAGE_SIZE)]
            else:
                dst = buf.at[slot, pl.ds(j * PAGE_SIZE, PAGE_SIZE), :]
            copies.append(pltpu.make_async_copy(src, dst, sems.at[slot, a]))
    return copies


def _paged_pipeline(pt_ref, layer, pc, specs, sems):
    bi, ci = pl.program_id(0), pl.program_id(1)
    nb, nc = pl.num_programs(0), pl.num_programs(1)
    step = bi * nc + ci
    slot = step % 2

    @pl.when(step == 0)
    def _():
        for cp in _paged_copies(pt_ref, bi, ci, slot, layer, pc, specs, sems):
            cp.start()

    @pl.when(step + 1 < nb * nc)
    def _():
        nxt = step + 1
        for cp in _paged_copies(pt_ref, nxt // nc, nxt % nc, 1 - slot, layer, pc, specs, sems):
            cp.start()

    for cp in _paged_copies(pt_ref, bi, ci, slot, layer, pc, specs, sems):
        cp.wait()
    return slot


def _paged_diff_kernel(pt_ref, q_ref, slope_ref, trow_ref, knew_ref, vnew_ref, kc_hbm, vc_hbm, o_ref,
                       kbuf, vbuf, sems, m_sc, l_sc, acc_sc, *, layer, pc, past_len):
    ci = pl.program_id(1)
    slot = _paged_pipeline(pt_ref, layer, pc, [(kc_hbm, kbuf, True), (vc_hbm, vbuf, False)], sems)

    @pl.when(ci == 0)
    def _():
        m_sc[...] = jnp.full(m_sc.shape, NEG_INF, F32)
        l_sc[...] = jnp.zeros(l_sc.shape, F32)
        acc_sc[...] = jnp.zeros(acc_sc.shape, F32)

    q = q_ref[0]
    slope = slope_ref[...]
    t_row = trow_ref[...]
    span = pc * PAGE_SIZE
    k_idx = lax.broadcasted_iota(jnp.int32, (1, KEY_SUB), 1)
    for off in range(0, span, KEY_SUB):
        keys = pl.ds(off, KEY_SUB)
        s = jnp.dot(q, kbuf[slot, :, keys].astype(BF16), preferred_element_type=F32)
        k_off = (ci * span + off + k_idx).astype(F32)
        rel = (t_row + float(past_len)) - k_off
        s = s - slope * rel
        _softmax_update(s, vbuf[slot, keys, :].astype(BF16), m_sc, l_sc, acc_sc)

    @pl.when(ci == pl.num_programs(1) - 1)
    def _():
        s_new = jnp.dot(q, knew_ref[0], preferred_element_type=F32)
        rel_new = t_row - lax.broadcasted_iota(jnp.int32, (1, PAGE_SIZE), 1).astype(F32)
        s_new = jnp.where(rel_new >= 0, s_new - slope * rel_new, NEG_INF)
        _softmax_update(s_new, vnew_ref[0], m_sc, l_sc, acc_sc)
        o_ref[0] = acc_sc[...] / l_sc[...]


def _paged_diff_attention(layer, q, slope, t_row, k_new_t, v_new, cache_k_t, cache_v, page_table, past_len):
    b, rows, feat = q.shape
    dv = cache_v.shape[-1]
    pc = PAGES_PER_STEP
    n_chunks = page_table.shape[1] // pc
    span = pc * PAGE_SIZE
    kern = functools.partial(_paged_diff_kernel, layer=layer, pc=pc, past_len=past_len)
    grid_spec = pltpu.PrefetchScalarGridSpec(
        num_scalar_prefetch=1,
        grid=(b, n_chunks),
        in_specs=[pl.BlockSpec((1, rows, feat), lambda bi, ci, pt: (bi, 0, 0)),
                  pl.BlockSpec((rows, 1), lambda bi, ci, pt: (0, 0)),
                  pl.BlockSpec((rows, 1), lambda bi, ci, pt: (0, 0)),
                  pl.BlockSpec((1, feat, PAGE_SIZE), lambda bi, ci, pt: (bi, 0, 0)),
                  pl.BlockSpec((1, PAGE_SIZE, dv), lambda bi, ci, pt: (bi, 0, 0)),
                  pl.BlockSpec(memory_space=pl.ANY),
                  pl.BlockSpec(memory_space=pl.ANY)],
        out_specs=pl.BlockSpec((1, rows, dv), lambda bi, ci, pt: (bi, 0, 0)),
        scratch_shapes=[pltpu.VMEM((2, feat, span), F32), pltpu.VMEM((2, span, dv), F32),
                        pltpu.SemaphoreType.DMA((2, 2)),
                        pltpu.VMEM((rows, 1), F32), pltpu.VMEM((rows, 1), F32), pltpu.VMEM((rows, dv), F32)])
    return pl.pallas_call(
        kern, grid_spec=grid_spec,
        out_shape=jax.ShapeDtypeStruct((b, rows, dv), F32),
        compiler_params=pltpu.CompilerParams(
            dimension_semantics=("arbitrary", "arbitrary"), vmem_limit_bytes=VMEM_LIMIT),
    )(page_table, q, slope, t_row, k_new_t, v_new, cache_k_t, cache_v)


def _paged_mla_kernel(pt_ref, ql_ref, qr_ref, trow_ref, cnew_ref, rnew_ref, cc_hbm, rc_hbm, o_ref,
                      cbuf, rbuf, sems, m_sc, l_sc, acc_sc, *, layer, pc):
    bi, ci = pl.program_id(0), pl.program_id(1)

    @pl.when((bi == 0) & (ci == 0))
    def _():
        rbuf[...] = jnp.zeros(rbuf.shape, F32)

    slot = _paged_pipeline(pt_ref, layer, pc, [(cc_hbm, cbuf, False), (rc_hbm, rbuf, True)], sems)

    @pl.when(ci == 0)
    def _():
        m_sc[...] = jnp.full(m_sc.shape, NEG_INF, F32)
        l_sc[...] = jnp.zeros(l_sc.shape, F32)
        acc_sc[...] = jnp.zeros(acc_sc.shape, F32)

    ql = ql_ref[0]
    qr = qr_ref[0]
    for off in range(0, pc * PAGE_SIZE, KEY_SUB):
        keys = pl.ds(off, KEY_SUB)
        ckv = cbuf[slot, keys, :].astype(BF16)
        s = lax.dot_general(ql, ckv, (((1,), (1,)), ((), ())), preferred_element_type=F32)
        s = s + jnp.dot(qr, rbuf[slot, :, keys].astype(BF16), preferred_element_type=F32)
        _softmax_update(s, ckv, m_sc, l_sc, acc_sc)

    @pl.when(ci == pl.num_programs(1) - 1)
    def _():
        c_new = cnew_ref[0]
        s_new = lax.dot_general(ql, c_new, (((1,), (1,)), ((), ())), preferred_element_type=F32)
        s_new = s_new + jnp.dot(qr, rnew_ref[0], preferred_element_type=F32)
        rel_new = trow_ref[...] - lax.broadcasted_iota(jnp.int32, (1, PAGE_SIZE), 1).astype(F32)
        s_new = jnp.where(rel_new >= 0, s_new, NEG_INF)
        _softmax_update(s_new, c_new, m_sc, l_sc, acc_sc)
        o_ref[0] = acc_sc[...] / l_sc[...]


def _paged_mla_attention(layer, q_lat, q_rope, t_row, c_new, r_new_t, cache_ckv, cache_krope_t, page_table):
    b, rows, lat = q_lat.shape
    pc = PAGES_PER_STEP
    n_chunks = page_table.shape[1] // pc
    span = pc * PAGE_SIZE
    kern = functools.partial(_paged_mla_kernel, layer=layer, pc=pc)
    grid_spec = pltpu.PrefetchScalarGridSpec(
        num_scalar_prefetch=1,
        grid=(b, n_chunks),
        in_specs=[pl.BlockSpec((1, rows, lat), lambda bi, ci, pt: (bi, 0, 0)),
                  pl.BlockSpec((1, rows, LANES), lambda bi, ci, pt: (bi, 0, 0)),
                  pl.BlockSpec((rows, 1), lambda bi, ci, pt: (0, 0)),
                  pl.BlockSpec((1, PAGE_SIZE, lat), lambda bi, ci, pt: (bi, 0, 0)),
                  pl.BlockSpec((1, LANES, PAGE_SIZE), lambda bi, ci, pt: (bi, 0, 0)),
                  pl.BlockSpec(memory_space=pl.ANY),
                  pl.BlockSpec(memory_space=pl.ANY)],
        out_specs=pl.BlockSpec((1, rows, lat), lambda bi, ci, pt: (bi, 0, 0)),
        scratch_shapes=[pltpu.VMEM((2, span, lat), F32), pltpu.VMEM((2, LANES, span), F32),
                        pltpu.SemaphoreType.DMA((2, 2)),
                        pltpu.VMEM((rows, 1), F32), pltpu.VMEM((rows, 1), F32), pltpu.VMEM((rows, lat), F32)])
    return pl.pallas_call(
        kern, grid_spec=grid_spec,
        out_shape=jax.ShapeDtypeStruct((b, rows, lat), F32),
        compiler_params=pltpu.CompilerParams(
            dimension_semantics=("arbitrary", "arbitrary"), vmem_limit_bytes=VMEM_LIMIT),
    )(page_table, q_lat, q_rope, t_row, c_new, r_new_t, cache_ckv, cache_krope_t)


def _rmsnorm(x, g):
    xf = x.astype(F32)
    y = xf * lax.rsqrt(jnp.mean(xf * xf, axis=-1, keepdims=True) + EPS)
    return (y * g.astype(F32)).astype(x.dtype)


def _l2norm(x):
    xf = x.astype(F32)
    return xf * lax.rsqrt(jnp.sum(xf * xf, axis=-1, keepdims=True) + EPS)


def _rope(x, pos):
    half = x.shape[-1] // 2
    inv = ROPE_THETA ** (-jnp.arange(half, dtype=F32) / half)
    ang = pos.astype(F32)[:, None] * inv
    ang = ang.reshape((pos.shape[0],) + (1,) * (x.ndim - 3) + (half,))
    cos, sin = jnp.cos(ang), jnp.sin(ang)
    xf = x.astype(F32)
    x1, x2 = xf[..., :half], xf[..., half:]
    return jnp.concatenate([x1 * cos - x2 * sin, x2 * cos + x1 * sin], axis=-1).astype(x.dtype)


def _alibi_slopes(n_heads):
    return [2.0 ** (-8.0 * (h + 1) / n_heads) for h in range(n_heads)]


def _causal_dwconv(x, buf, w):
    width, length = w.shape[0], x.shape[1]
    xx = jnp.concatenate([buf.astype(x.dtype), x], axis=1)
    y = w[0] * xx[:, :length]
    for j in range(1, width):
        y = y + w[j] * xx[:, j:j + length]
    return y, xx[:, length:]


def _gated_delta_rule(q, k, v, g, beta, s0):
    b, length, h, _ = q.shape
    c = math.gcd(length, GDN_CHUNK)
    n = length // c

    def chunks(t):
        t = t.reshape((b, n, c, h) + t.shape[3:])
        return jnp.moveaxis(t, (1, 3), (0, 2))

    qc, kc, vc, bc = chunks(q), chunks(k), chunks(v), chunks(beta)
    gc = jnp.cumsum(chunks(g), axis=-1)
    tril = jnp.tril(jnp.ones((c, c), dtype=bool))
    strict = jnp.tril(jnp.ones((c, c), dtype=bool), k=-1)
    diff = gc[..., :, None] - gc[..., None, :]
    decay = jnp.where(tril, jnp.exp(jnp.where(tril, diff, 0.0)), 0.0)
    k_beta = kc * bc[..., None]
    a = jnp.where(strict, jnp.einsum('nbhid,nbhjd->nbhij', k_beta, kc) * decay, 0.0)
    lmat = a + jnp.eye(c, dtype=F32)
    u = lax.linalg.triangular_solve(lmat, vc * bc[..., None], left_side=True, lower=True, unit_diagonal=True)
    w = lax.linalg.triangular_solve(lmat, k_beta * jnp.exp(gc)[..., None], left_side=True, lower=True,
                                    unit_diagonal=True)
    qk = jnp.where(tril, jnp.einsum('nbhid,nbhjd->nbhij', qc, kc) * decay, 0.0)

    def step(s, xs):
        q_i, k_i, u_i, w_i, g_i, qk_i = xs
        v_new = u_i - jnp.einsum('bhcd,bhdv->bhcv', w_i, s)
        o = (jnp.einsum('bhcd,bhdv->bhcv', q_i * jnp.exp(g_i)[..., None], s)
             + jnp.einsum('bhij,bhjv->bhiv', qk_i, v_new))
        g_last = g_i[..., -1]
        k_dec = k_i * jnp.exp(g_last[..., None] - g_i)[..., None]
        s = s * jnp.exp(g_last)[..., None, None] + jnp.einsum('bhcd,bhcv->bhdv', k_dec, v_new)
        return s, o

    s_final, o = lax.scan(step, s0, (qc, kc, u, w, gc, qk))
    o = jnp.moveaxis(o, (0, 2), (1, 3)).reshape(b, length, h, -1)
    return o, s_final


SUBLANES = 8


def _conv_gate_rows(x, prev, w_ref):
    row = lax.broadcasted_iota(jnp.int32, x.shape, 1)
    x1 = jnp.where(row < 1, pltpu.roll(prev, 1, axis=1), pltpu.roll(x, 1, axis=1))
    x2 = jnp.where(row < 2, pltpu.roll(prev, 2, axis=1), pltpu.roll(x, 2, axis=1))
    return w_ref[0:1, :] * x2 + w_ref[1:2, :] * x1 + w_ref[2:3, :] * x


def _ffn_act_kernel(g_ref, v_ref, gh_ref, vh_ref, gs_ref, vs_ref, wg_ref, wv_ref, o_ref, *,
                    tiles_per_seq, per_group_state):
    groups = g_ref.shape[0]

    def act(xg, pg, xv, pv):
        yg = _conv_gate_rows(xg, pg, wg_ref)
        yv = _conv_gate_rows(xv, pv, wv_ref)
        out = yg / (1.0 + jnp.exp(-yg)) * yv
        return out.reshape(out.shape[0] * SUBLANES, out.shape[2])

    if per_group_state:
        out = act(g_ref[...], gs_ref[...], v_ref[...], vs_ref[...])
    else:
        first = pl.program_id(0) % tiles_per_seq == 0
        pg0 = jnp.where(first, gs_ref[...], gh_ref[...])
        pv0 = jnp.where(first, vs_ref[...], vh_ref[...])
        out = act(g_ref[pl.ds(0, 1)], pg0, v_ref[pl.ds(0, 1)], pv0)
        if groups > 1:
            rest = pl.ds(1, groups - 1)
            head = pl.ds(0, groups - 1)
            out = jnp.concatenate([out, act(g_ref[rest], g_ref[head], v_ref[rest], v_ref[head])], axis=0)
    o_ref[...] = out.astype(o_ref.dtype)


def _ffn_activation(up, conv_w, state, n_seq, length, *, tn=512):
    tokens = n_seq * length
    up3 = up.reshape(tokens // SUBLANES, SUBLANES, 2 * D_FF)
    st8 = jnp.pad(state.astype(F32), ((0, 0), (SUBLANES - (FFN_CONV - 1), 0), (0, 0)))
    val_block = D_FF // tn
    per_group_state = length == SUBLANES
    if per_group_state:
        groups, tiles_per_seq = n_seq, 1
        state_map_g = lambda i, j: (0, 0, j)
        state_map_v = lambda i, j: (0, 0, j + val_block)
        state_rows = n_seq
    else:
        groups = min(64, length // SUBLANES)
        tiles_per_seq = length // (groups * SUBLANES)
        assert tiles_per_seq * groups * SUBLANES == length
        state_map_g = lambda i, j: (i // tiles_per_seq, 0, j)
        state_map_v = lambda i, j: (i // tiles_per_seq, 0, j + val_block)
        state_rows = 1
    n_tiles = tokens // (groups * SUBLANES)
    cur = (groups, SUBLANES, tn)
    halo = (1, SUBLANES, tn)
    kern = functools.partial(_ffn_act_kernel, tiles_per_seq=tiles_per_seq, per_group_state=per_group_state)
    return pl.pallas_call(
        kern,
        grid=(n_tiles, D_FF // tn),
        in_specs=[pl.BlockSpec(cur, lambda i, j: (i, 0, j)),
                  pl.BlockSpec(cur, lambda i, j: (i, 0, j + val_block)),
                  pl.BlockSpec(halo, lambda i, j: (jnp.maximum(i * groups - 1, 0), 0, j)),
                  pl.BlockSpec(halo, lambda i, j: (jnp.maximum(i * groups - 1, 0), 0, j + val_block)),
                  pl.BlockSpec((state_rows, SUBLANES, tn), state_map_g),
                  pl.BlockSpec((state_rows, SUBLANES, tn), state_map_v),
                  pl.BlockSpec((FFN_CONV, tn), lambda i, j: (0, j)),
                  pl.BlockSpec((FFN_CONV, tn), lambda i, j: (0, j + val_block))],
        out_specs=pl.BlockSpec((groups * SUBLANES, tn), lambda i, j: (i, j)),
        out_shape=jax.ShapeDtypeStruct((tokens, D_FF), BF16),
        compiler_params=pltpu.CompilerParams(
            dimension_semantics=("parallel", "parallel"), vmem_limit_bytes=VMEM_LIMIT),
    )(up3, up3, up3, up3, st8, st8, conv_w, conv_w)


def _merge_kernel(ya_ref, yb_ref, yc_ref, yd_ref, w_ref, ga_ref, gb_ref, gc_ref, gd_ref, o_ref):
    acc = None
    for n, (y_ref, g_ref) in enumerate(((ya_ref, ga_ref), (yb_ref, gb_ref), (yc_ref, gc_ref), (yd_ref, gd_ref))):
        z = jnp.dot(y_ref[...], w_ref[n], preferred_element_type=F32)
        term = z / (1.0 + jnp.exp(-g_ref[...]))
        acc = term if acc is None else acc + term
    o_ref[...] = acc.astype(o_ref.dtype)


def _merge_branches(ys, proj, w_branch, *, tm=512, tn=512):
    tokens = proj.shape[0]
    tm = min(tm, tokens)
    gate_block = OFF_GATES // tn
    per_branch = D_MODEL // tn
    y_spec = pl.BlockSpec((tm, BRANCH_WIDTH), lambda i, j: (i, 0))
    gate_specs = [pl.BlockSpec((tm, tn), functools.partial(lambda i, j, n: (i, gate_block + n * per_branch + j), n=n))
                  for n in range(N_BRANCH)]
    return pl.pallas_call(
        _merge_kernel,
        grid=(tokens // tm, D_MODEL // tn),
        in_specs=[y_spec] * N_BRANCH + [pl.BlockSpec((N_BRANCH, BRANCH_WIDTH, tn), lambda i, j: (0, 0, j))]
        + gate_specs,
        out_specs=pl.BlockSpec((tm, tn), lambda i, j: (i, j)),
        out_shape=jax.ShapeDtypeStruct((tokens, D_MODEL), BF16),
        compiler_params=pltpu.CompilerParams(
            dimension_semantics=("parallel", "parallel"), vmem_limit_bytes=VMEM_LIMIT),
    )(*[y.astype(BF16) for y in ys], w_branch, proj, proj, proj, proj)


S5_WIDTH = S5_GROUPS * S5_STATE
S5_LANE_CHUNK = 1024


def _s5_scan_kernel(bu_ref, a_ref, x0_ref, o_ref, last_ref, x_sc, *, rows):
    ti = pl.program_id(1)

    @pl.when(ti == 0)
    def _():
        x_sc[...] = x0_ref[0]

    for c0 in range(0, S5_WIDTH, S5_LANE_CHUNK):
        re = pl.ds(c0, S5_LANE_CHUNK)
        im = pl.ds(S5_WIDTH + c0, S5_LANE_CHUNK)
        a_re = a_ref[:, re]
        a_im = a_ref[:, im]

        def body(r, carry, re=re, im=im, a_re=a_re, a_im=a_im):
            x_re, x_im = carry
            row = pl.ds(r, 1)
            n_re = a_re * x_re - a_im * x_im + bu_ref[row, re]
            n_im = a_re * x_im + a_im * x_re + bu_ref[row, im]
            o_ref[row, re] = n_re
            o_ref[row, im] = n_im
            return n_re, n_im

        x_re, x_im = lax.fori_loop(0, rows, body, (x_sc[:, re], x_sc[:, im]), unroll=8)
        x_sc[:, re] = x_re
        x_sc[:, im] = x_im

    @pl.when(ti == pl.num_programs(1) - 1)
    def _():
        last_ref[0] = x_sc[...]


def _s5_scan(bu, a, x0, n_seq, length):
    rows = min(length, 256)
    width = bu.shape[1]
    tiles = length // rows
    return pl.pallas_call(
        functools.partial(_s5_scan_kernel, rows=rows),
        grid=(n_seq, tiles),
        in_specs=[pl.BlockSpec((rows, width), lambda s, t: (s * tiles + t, 0)),
                  pl.BlockSpec((1, width), lambda s, t: (0, 0)),
                  pl.BlockSpec((1, 1, width), lambda s, t: (s, 0, 0))],
        out_specs=[pl.BlockSpec((rows, width), lambda s, t: (s * tiles + t, 0)),
                   pl.BlockSpec((1, 1, width), lambda s, t: (s, 0, 0))],
        out_shape=[jax.ShapeDtypeStruct(bu.shape, F32), jax.ShapeDtypeStruct((n_seq, 1, width), F32)],
        scratch_shapes=[pltpu.VMEM((1, width), F32)],
        compiler_params=pltpu.CompilerParams(
            dimension_semantics=("parallel", "arbitrary"), vmem_limit_bytes=VMEM_LIMIT),
    )(bu, a, x0)


def _s5(u, x0_re, x0_im, p):
    b, length, _ = u.shape
    tokens = b * length
    lam = lax.complex(p['s5_a_re'].astype(F32), p['s5_a_im'].astype(F32))
    dt = jnp.exp(p['s5_log_dt'].astype(F32))[:, None]
    lam_bar = jnp.exp(lam * dt)
    b_bar = ((lam_bar - 1.0) / lam)[..., None] * lax.complex(p['s5_b_re'].astype(F32), p['s5_b_im'].astype(F32))
    eye = jnp.eye(S5_GROUPS, dtype=F32)

    def in_mat(m):
        return (m[:, None] * eye[:, :, None, None]).transpose(0, 3, 1, 2).reshape(BRANCH_WIDTH, S5_WIDTH)

    def out_mat(m):
        return (m[:, None] * eye[:, :, None, None]).transpose(0, 3, 1, 2).reshape(S5_WIDTH, BRANCH_WIDTH)

    b_mat = jnp.concatenate([in_mat(jnp.real(b_bar)), in_mat(jnp.imag(b_bar))], axis=1)
    c_mat = jnp.concatenate([out_mat(p['s5_c_re'].astype(F32)), -out_mat(p['s5_c_im'].astype(F32))], axis=0)
    a = jnp.concatenate([jnp.real(lam_bar).reshape(1, S5_WIDTH), jnp.imag(lam_bar).reshape(1, S5_WIDTH)], axis=1)
    x0 = jnp.concatenate([x0_re.astype(F32).reshape(b, 1, S5_WIDTH), x0_im.astype(F32).reshape(b, 1, S5_WIDTH)],
                         axis=2)
    u2 = u.reshape(tokens, BRANCH_WIDTH)
    bu = _mm(u2, b_mat)
    states, last = _s5_scan(bu, a, x0, b, length)
    y = _mm(states, c_mat) + p['s5_d'].astype(F32) * u2
    v = jax.nn.gelu(y)
    glu = _mm(v, p['s5_w_glu_bf']).reshape(b, length, 2 * BRANCH_WIDTH)
    glu_a, glu_b = jnp.split(glu, 2, axis=-1)
    last_re = last[:, 0, :S5_WIDTH].reshape(b, S5_GROUPS, S5_STATE)
    last_im = last[:, 0, S5_WIDTH:].reshape(b, S5_GROUPS, S5_STATE)
    return glu_a * jax.nn.sigmoid(glu_b), last_re.astype(x0_re.dtype), last_im.astype(x0_im.dtype)


def _attend_prompt_diff(dq, dk, d_v):
    q = (dq * DIFF_HEAD_DIM ** -0.5).transpose(0, 2, 3, 1, 4)
    k = dk.transpose(0, 2, 1, 3)
    out = _prompt_attention(q, k, d_v, _alibi_slopes(DIFF_HEADS))
    return out.transpose(0, 3, 1, 2, 4)


def _attend_prompt_mla(q_cat, k_cat, ckv):
    pad = 3 * LANES - q_cat.shape[-1]
    q = (q_cat * (MLA_NOPE + MLA_ROPE) ** -0.5).transpose(0, 2, 1, 3)[:, None]
    q = jnp.pad(q, ((0, 0),) * 4 + ((0, pad),))
    k = jnp.pad(k_cat, ((0, 0), (0, 0), (0, pad)))[:, None]
    out = _prompt_attention(q, k, ckv, None)
    return out[:, 0].transpose(0, 2, 1, 3)


def _pad_rows(x, n, axis):
    pad = [(0, 0)] * x.ndim
    pad[axis] = (0, n - x.shape[axis])
    return jnp.pad(x, pad)


def _attend_paged_diff(layer, dq, dk, d_v, cache_k_t, cache_v, page_table):
    b, t = dq.shape[:2]
    d = DIFF_HEAD_DIM
    past_len = page_table.shape[1] * PAGE_SIZE
    qs = (dq * d ** -0.5).transpose(0, 2, 1, 3, 4).reshape(b, 2, t * DIFF_HEADS, d)
    zeros = jnp.zeros_like(qs[:, 0])
    q = jnp.concatenate([jnp.concatenate([qs[:, 0], zeros], axis=-1),
                         jnp.concatenate([zeros, qs[:, 1]], axis=-1)], axis=1).astype(BF16)
    slope = jnp.tile(jnp.asarray(_alibi_slopes(DIFF_HEADS), F32), 2 * t)[:, None]
    t_row = jnp.tile(jnp.repeat(jnp.arange(t, dtype=F32), DIFF_HEADS), 2)[:, None]
    k_new_t = _pad_rows(dk.reshape(b, t, 2 * d), PAGE_SIZE, 1).transpose(0, 2, 1).astype(BF16)
    v_new = _pad_rows(d_v, PAGE_SIZE, 1).astype(BF16)
    out = _paged_diff_attention(layer, q, slope, t_row, k_new_t, v_new, cache_k_t, cache_v, page_table, past_len)
    return out.reshape(b, 2, t, DIFF_HEADS, 2 * d).transpose(0, 2, 1, 3, 4)


def _attend_paged_mla(layer, q_lat, q_rope, ckv, krope, cache_ckv, cache_krope_t, page_table):
    b, t = q_lat.shape[:2]
    scale = (MLA_NOPE + MLA_ROPE) ** -0.5
    ql = (q_lat * scale).reshape(b, t * MLA_HEADS, MLA_KV_LORA).astype(BF16)
    qr = _pad_rows((q_rope * scale).reshape(b, t * MLA_HEADS, MLA_ROPE), LANES, 2).astype(BF16)
    t_row = jnp.repeat(jnp.arange(t, dtype=F32), MLA_HEADS)[:, None]
    c_new = _pad_rows(ckv, PAGE_SIZE, 1).astype(BF16)
    r_new_t = _pad_rows(_pad_rows(krope, PAGE_SIZE, 1), LANES, 2).transpose(0, 2, 1).astype(BF16)
    out = _paged_mla_attention(layer, ql, qr, t_row, c_new, r_new_t, cache_ckv, cache_krope_t, page_table)
    return out.reshape(b, t, MLA_HEADS, MLA_KV_LORA)


def _layer(i, x, mod, pos, st, p, paged):
    b, length, _ = x.shape
    tokens = b * length
    sh_m, sc_m, gt_m, sh_f, sc_f, gt_f = jnp.split(mod[:, None, :], 6, axis=-1)
    h = _rmsnorm(x, p['norm_mix']) * (1 + sc_m) + sh_m
    proj = _mm(h.reshape(tokens, D_MODEL), p['w_in_bf']).reshape(b, length, IN_WIDTH)
    g_qkv = proj[..., OFF_QKV:OFF_Z]
    g_z = proj[..., OFF_Z:OFF_DQ]
    d_q = proj[..., OFF_DQ:OFF_DK]
    d_k = proj[..., OFF_DK:OFF_DV]
    d_v = proj[..., OFF_DV:OFF_SU]
    s_u = proj[..., OFF_SU:OFF_MQA]
    m_qa = proj[..., OFF_MQA:OFF_MKVA]
    m_kva = proj[..., OFF_MKVA:OFF_MKR]
    m_kr = proj[..., OFF_MKR:OFF_GB]
    g_b = proj[..., OFF_GB:OFF_GA]
    g_a = proj[..., OFF_GA:OFF_GA + GDN_HEADS]

    qkv, gdn_conv_new = _causal_dwconv(g_qkv, st['gdn_conv'], p['gdn_conv_w'])
    qkv = jax.nn.silu(qkv).reshape(b, length, 3, GDN_HEADS, GDN_HEAD_DIM)
    gq = _l2norm(qkv[:, :, 0]) * GDN_HEAD_DIM ** -0.5
    gk = _l2norm(qkv[:, :, 1])
    gv = qkv[:, :, 2].astype(F32)
    beta = jax.nn.sigmoid(g_b.astype(F32))
    log_decay = -jnp.exp(p['gdn_a_log'].astype(F32)) * jax.nn.softplus(g_a.astype(F32) + p['gdn_dt_bias'].astype(F32))
    o_a, gdn_state_new = _gated_delta_rule(gq, gk, gv, log_decay, beta, st['gdn'].astype(F32))
    o_a = _rmsnorm(o_a, p['gdn_norm']) * jax.nn.silu(g_z.reshape(b, length, GDN_HEADS, GDN_HEAD_DIM).astype(F32))
    y_a = o_a.reshape(b, length, BRANCH_WIDTH)

    dq = _rmsnorm(d_q.reshape(b, length, DIFF_HEADS, 2, DIFF_HEAD_DIM).swapaxes(2, 3), p['diff_q_norm'])
    dk = _rmsnorm(d_k.reshape(b, length, 2, DIFF_HEAD_DIM), p['diff_k_norm'])
    lam_init = 0.8 - 0.6 * math.exp(-0.3 * i)
    lam = (jnp.exp(jnp.sum(p['diff_lq1'].astype(F32) * p['diff_lk1'].astype(F32)))
           - jnp.exp(jnp.sum(p['diff_lq2'].astype(F32) * p['diff_lk2'].astype(F32))) + lam_init)
    if paged is None:
        att_b = _attend_prompt_diff(dq, dk, d_v)
    else:
        att_b = _attend_paged_diff(i, dq, dk, d_v, paged['diff_k_t'], paged['diff_v'], paged['page_table'])
    o_b = att_b[:, :, 0] - lam * att_b[:, :, 1]
    o_b = _rmsnorm(o_b, p['diff_subln']) * (1.0 - lam_init)
    y_b = o_b.reshape(b, length, BRANCH_WIDTH)

    y_c, s5_re_new, s5_im_new = _s5(s_u, st['s5_re'], st['s5_im'], p)

    mq = _mm(_rmsnorm(m_qa, p['mla_q_a_norm']).reshape(tokens, MLA_Q_LORA), p['mla_w_qb_bf'])
    mq = _rmsnorm(mq.reshape(b, length, MLA_HEADS, MLA_NOPE + MLA_ROPE), p['mla_q_norm'])
    q_rope = _rope(mq[..., MLA_NOPE:], pos)
    q_lat = jnp.einsum('blhd,rhd->blhr', mq[..., :MLA_NOPE], p['mla_w_kvb'][..., :MLA_NOPE])
    ckv = _rmsnorm(m_kva, p['mla_kv_norm'])
    krope = _rope(_rmsnorm(m_kr, p['mla_k_norm']), pos)
    if paged is None:
        q_cat = jnp.concatenate([q_lat, q_rope], axis=-1)
        k_cat = jnp.concatenate([ckv, krope], axis=-1)
        att_d = _attend_prompt_mla(q_cat, k_cat, ckv)
    else:
        att_d = _attend_paged_mla(i, q_lat, q_rope, ckv, krope, paged['mla_ckv'], paged['mla_krope_t'],
                                  paged['page_table'])
    y_d = jnp.einsum('blhr,rhv->blhv', att_d, p['mla_w_kvb'][..., MLA_NOPE:]).reshape(b, length, BRANCH_WIDTH)

    merged = _merge_branches([y.reshape(tokens, BRANCH_WIDTH) for y in (y_a, y_b, y_c, y_d)],
                             proj.reshape(tokens, IN_WIDTH), p['w_branch_bf'])
    x = x + gt_m * _mm(merged, p['w_out_bf']).reshape(b, length, D_MODEL)

    hf = _rmsnorm(x, p['norm_ffn']) * (1 + sc_f) + sh_f
    up = _mm(hf.reshape(tokens, D_MODEL), p['ffn_w_up_bf'])
    ffn_conv_new = up.reshape(b, length, 2 * D_FF)[:, length - (FFN_CONV - 1):]
    act = _ffn_activation(up, p['ffn_conv_w'], st['ffn_conv'], b, length)
    x = x + gt_f * _mm(act, p['ffn_w_down_bf'], tm=512).reshape(b, length, D_MODEL)

    new = dict(diff_k=dk, diff_v=d_v, mla_ckv=ckv, mla_krope=krope,
               gdn=gdn_state_new.astype(st['gdn'].dtype), gdn_conv=gdn_conv_new,
               s5_re=s5_re_new, s5_im=s5_im_new, ffn_conv=ffn_conv_new)
    return x, new


STATE_NAMES = ('diff_k', 'diff_v', 'mla_ckv', 'mla_krope', 'gdn', 'gdn_conv', 's5_re', 's5_im', 'ffn_conv')


def _reorder_w_in(w):
    pad = jnp.zeros((w.shape[0], OFF_GATES - (OFF_GA + GDN_HEADS)), w.dtype)
    return jnp.concatenate([w[:, :2048], w[:, 2056:3976], w[:, 3976:4008], w[:, 2048:2056], pad, w[:, 4008:]],
                           axis=1)


def kernel(x_prompt, x_sample, cache_diff_k, cache_diff_v, cache_mla_ckv, cache_mla_krope, state_gdn, state_gdn_conv, state_s5_re, state_s5_im, state_ffn_conv, page_table, c_prompt, c_sample, w_ada, b_ada, norm_mix, norm_ffn, w_in, gdn_conv_w, gdn_a_log, gdn_dt_bias, gdn_norm, diff_q_norm, diff_k_norm, diff_lq1, diff_lk1, diff_lq2, diff_lk2, diff_subln, s5_a_re, s5_a_im, s5_log_dt, s5_b_re, s5_b_im, s5_c_re, s5_c_im, s5_d, s5_w_glu, mla_q_a_norm, mla_w_qb, mla_q_norm, mla_kv_norm, mla_k_norm, mla_w_kvb, w_branch, w_out, ffn_w_up, ffn_conv_w, ffn_w_down):
    weights = dict(w_ada=w_ada, b_ada=b_ada, norm_mix=norm_mix, norm_ffn=norm_ffn, w_in=w_in,
                   gdn_conv_w=gdn_conv_w, gdn_a_log=gdn_a_log, gdn_dt_bias=gdn_dt_bias, gdn_norm=gdn_norm,
                   diff_q_norm=diff_q_norm, diff_k_norm=diff_k_norm, diff_lq1=diff_lq1, diff_lk1=diff_lk1,
                   diff_lq2=diff_lq2, diff_lk2=diff_lk2, diff_subln=diff_subln,
                   s5_a_re=s5_a_re, s5_a_im=s5_a_im, s5_log_dt=s5_log_dt, s5_b_re=s5_b_re, s5_b_im=s5_b_im,
                   s5_c_re=s5_c_re, s5_c_im=s5_c_im, s5_d=s5_d, s5_w_glu=s5_w_glu,
                   mla_q_a_norm=mla_q_a_norm, mla_w_qb=mla_w_qb, mla_q_norm=mla_q_norm,
                   mla_kv_norm=mla_kv_norm, mla_k_norm=mla_k_norm, mla_w_kvb=mla_w_kvb,
                   w_branch=w_branch, w_out=w_out, ffn_w_up=ffn_w_up, ffn_conv_w=ffn_conv_w, ffn_w_down=ffn_w_down)
    n_prompt, n_sample = x_prompt.shape[0], x_sample.shape[0]
    past_len = page_table.shape[1] * PAGE_SIZE
    pos_p = jnp.arange(x_prompt.shape[1], dtype=jnp.int32)
    pos_s = past_len + jnp.arange(x_sample.shape[1], dtype=jnp.int32)

    n_pool = cache_diff_k.shape[1]
    paged = dict(
        diff_k_t=cache_diff_k.transpose(0, 1, 3, 4, 2).reshape(DEPTH, n_pool, 2 * DIFF_HEAD_DIM, PAGE_SIZE),
        diff_v=cache_diff_v,
        mla_ckv=cache_mla_ckv,
        mla_krope_t=cache_mla_krope.transpose(0, 1, 3, 2),
        page_table=page_table)

    c_all = jnp.concatenate([c_prompt, c_sample], axis=0)
    n_mod = -(-c_all.shape[0] // 16) * 16
    c_act = _pad_rows(jax.nn.silu(c_all), n_mod, 0)

    xp, xs = x_prompt, x_sample
    rows_p = {name: [] for name in STATE_NAMES}
    rows_s = {name: [] for name in STATE_NAMES}
    for i in range(DEPTH):
        p = {name: w[i] for name, w in weights.items()}
        p['w_in_bf'] = _reorder_w_in(p['w_in']).astype(BF16)
        for name in ('s5_w_glu', 'mla_w_qb', 'w_branch', 'w_out', 'ffn_w_up', 'ffn_w_down'):
            p[name + '_bf'] = p[name].astype(BF16)
        mod = _mm(c_act, p['w_ada'].astype(BF16))[:n_prompt + n_sample] + p['b_ada']
        zeros_p = dict(gdn=jnp.zeros((n_prompt, GDN_HEADS, GDN_HEAD_DIM, GDN_HEAD_DIM), F32),
                       gdn_conv=jnp.zeros((n_prompt, GDN_CONV - 1, QKV_WIDTH), F32),
                       s5_re=jnp.zeros((n_prompt, S5_GROUPS, S5_STATE), F32),
                       s5_im=jnp.zeros((n_prompt, S5_GROUPS, S5_STATE), F32),
                       ffn_conv=jnp.zeros((n_prompt, FFN_CONV - 1, 2 * D_FF), F32))
        xp, new_p = _layer(i, xp, mod[:n_prompt], pos_p, zeros_p, p, None)
        st_s = dict(gdn=state_gdn[i], gdn_conv=state_gdn_conv[i], s5_re=state_s5_re[i],
                    s5_im=state_s5_im[i], ffn_conv=state_ffn_conv[i])
        xs, new_s = _layer(i, xs, mod[n_prompt:], pos_s, st_s, p, paged)
        for name in STATE_NAMES:
            rows_p[name].append(new_p[name])
            rows_s[name].append(new_s[name])
    out_p = {name: jnp.stack(v, axis=0) for name, v in rows_p.items()}
    out_s = {name: jnp.stack(v, axis=0) for name, v in rows_s.items()}
    return (xp, xs,
            out_p['diff_k'], out_p['diff_v'], out_p['mla_ckv'], out_p['mla_krope'], out_p['gdn'],
            out_p['gdn_conv'], out_p['s5_re'], out_p['s5_im'], out_p['ffn_conv'],
            out_s['diff_k'], out_s['diff_v'], out_s['mla_ckv'], out_s['mla_krope'], out_s['gdn'],
            out_s['gdn_conv'], out_s['s5_re'], out_s['s5_im'], out_s['ffn_conv'])
```
